```python
import math
import jax, jax.numpy as jnp
from jax import lax
import numpy as np

D_MODEL = 1024
BATCH = 4
SEQ = 4096
DEPTH = 2

D_MIX = D_MODEL
ATTN_WIDTH = 3 * D_MIX // 8
LRU_WIDTH = 3 * D_MIX // 8
S5_WIDTH = D_MIX - ATTN_WIDTH - LRU_WIDTH

HEAD_DIM = 64
N_ATTN_HEADS = ATTN_WIDTH // HEAD_DIM
DILATED_PAIRS = ((128, 1), (512, 4), (2048, 16))
ATTN_BLOCK = 128
ROPE_THETA = 10000.0

LRU_HEAD = 64
N_LRU_HEADS = LRU_WIDTH // LRU_HEAD
LRU_CONV = 4
LRU_C = 8.0

S5_GROUP = 16
N_S5_GROUPS = S5_WIDTH // S5_GROUP
S5_STATE = 64

D_FF = 3 * D_MODEL
FFN_CONV = 3

DEEPNORM_ALPHA = (2 * DEPTH) ** 0.25
DEEPNORM_BETA = (8 * DEPTH) ** -0.25
LN_EPS = 1e-5
RMS_EPS = 1e-6

Q_OFF = 0
K_OFF = ATTN_WIDTH
V_OFF = 2 * ATTN_WIDTH
LRU_X_OFF = 3 * ATTN_WIDTH
LRU_G_OFF = LRU_X_OFF + LRU_WIDTH
S5_OFF = LRU_G_OFF + LRU_WIDTH
D_IN = S5_OFF + S5_WIDTH

kernel_name = 'hybrid_dilated_attn_rglru_s5_deepnorm'


def _layer_norm(x, g, b):
    xf = x.astype(jnp.float32)
    mu = jnp.mean(xf, axis=-1, keepdims=True)
    var = jnp.mean(jnp.square(xf - mu), axis=-1, keepdims=True)
    y = (xf - mu) * lax.rsqrt(var + LN_EPS) * g.astype(jnp.float32) + b.astype(jnp.float32)
    return y.astype(x.dtype)


def _rms_norm(x, g):
    xf = x.astype(jnp.float32)
    ms = jnp.mean(jnp.square(xf), axis=-1, keepdims=True)
    return xf * lax.rsqrt(ms + RMS_EPS) * g.astype(jnp.float32)


def _causal_dwconv(x, w, b):
    k = w.shape[0]
    y = lax.conv_general_dilated(
        x, w[:, None, :].astype(x.dtype), window_strides=(1,), padding=((k - 1, 0),),
        dimension_numbers=('NWC', 'WIO', 'NWC'), feature_group_count=x.shape[-1])
    return y + b.astype(x.dtype)


def _rope(x):
    s = x.shape[1]
    half = HEAD_DIM // 2
    pos = jnp.arange(s, dtype=jnp.float32)
    inv = ROPE_THETA ** (-jnp.arange(half, dtype=jnp.float32) * 2.0 / HEAD_DIM)
    ang = pos[:, None] * inv[None, :]
    cos = jnp.cos(ang)[None, :, None, :]
    sin = jnp.sin(ang)[None, :, None, :]
    xf = x.astype(jnp.float32)
    x1, x2 = xf[..., :half], xf[..., half:]
    return jnp.concatenate([x1 * cos - x2 * sin, x2 * cos + x1 * sin], axis=-1)


def _dilated_branch(q, k, v, window, dilation):
    b, s, nh, hd = q.shape
    m = s // dilation
    nback = window // dilation
    nb = -(-m // ATTN_BLOCK)
    mp = nb * ATTN_BLOCK

    def strided(t, left):
        t = t.reshape(b, m, dilation, nh, hd)
        return jnp.pad(t, ((0, 0), (left, mp - m), (0, 0), (0, 0), (0, 0)))

    def key_blocks(t):
        t = strided(t, ATTN_BLOCK).reshape(b, nb + 1, ATTN_BLOCK, dilation, nh, hd)
        return jnp.concatenate([t[:, :-1], t[:, 1:]], axis=2)

    qb = strided(q, 0).reshape(b, nb, ATTN_BLOCK, dilation, nh, hd)
    kb = key_blocks(k)
    vb = key_blocks(v)
    scores = jnp.einsum('bnqchd,bnkchd->bnchqk', qb, kb) * (hd ** -0.5)
    qi = jnp.arange(ATTN_BLOCK)[:, None]
    ki = jnp.arange(2 * ATTN_BLOCK)[None, :]
    dist = qi + ATTN_BLOCK - ki
    blk = jnp.arange(nb)[:, None, None]
    valid = (dist >= 0) & (dist <= nback) & ((blk - 1) * ATTN_BLOCK + ki >= 0)
    scores = jnp.where(valid[None, :, None, None], scores, -jnp.inf)
    lse = jax.nn.logsumexp(scores, axis=-1)
    probs = jnp.exp(scores - lse[..., None])
    out = jnp.einsum('bnchqk,bnkchd->bnqchd', probs, vb)
    out = out.reshape(b, mp, dilation, nh, hd)[:, :m].reshape(b, s, nh, hd)
    lse = jnp.transpose(lse, (0, 1, 4, 2, 3)).reshape(b, mp, dilation, nh)[:, :m].reshape(b, s, nh)
    return out, lse


def _dilated_attention(q, k, v):
    q = _rope(q)
    k = _rope(k)
    v = v.astype(jnp.float32)
    outs, lses = [], []
    for window, dilation in DILATED_PAIRS:
        o, l = _dilated_branch(q, k, v, window, dilation)
        outs.append(o)
        lses.append(l)
    wts = jax.nn.softmax(jnp.stack(lses, axis=0), axis=0)
    return jnp.sum(wts[..., None] * jnp.stack(outs, axis=0), axis=0)


def _linear_combine(e1, e2):
    a1, b1 = e1
    a2, b2 = e2
    return (a1 * a2, a2 * b1 + b2)


def _complex_combine(e1, e2):
    ar1, ai1, br1, bi1 = e1
    ar2, ai2, br2, bi2 = e2
    return (ar2 * ar1 - ai2 * ai1,
            ar2 * ai1 + ai2 * ar1,
            ar2 * br1 - ai2 * bi1 + br2,
            ar2 * bi1 + ai2 * br1 + bi2)


def _rg_lru_branch(xr, gate, conv_w, conv_b, wr, br, wi, bi, lam):
    b, s, _ = xr.shape
    f32 = jnp.float32
    xc = _causal_dwconv(xr, conv_w, conv_b).astype(f32)
    xh = xc.reshape(b, s, N_LRU_HEADS, LRU_HEAD)
    r = jax.nn.sigmoid(jnp.einsum('bshi,hij->bshj', xh, wr.astype(f32)).reshape(b, s, LRU_WIDTH) + br.astype(f32))
    i = jax.nn.sigmoid(jnp.einsum('bshi,hij->bshj', xh, wi.astype(f32)).reshape(b, s, LRU_WIDTH) + bi.astype(f32))
    log_a = -LRU_C * r * jax.nn.softplus(-lam.astype(f32))
    a = jnp.exp(log_a)
    u = jnp.sqrt(-jnp.expm1(2.0 * log_a)) * (i * xc)
    _, h = lax.associative_scan(_linear_combine, (a, u), axis=1)
    return h * jax.nn.gelu(gate.astype(f32))


def _s5_branch(u, a_re, a_im, b_re, b_im, c_re, c_im, d, log_step, w_glu, b_glu):
    f32 = jnp.float32
    bsz, s, _ = u.shape
    uf = u.astype(f32).reshape(bsz, s, N_S5_GROUPS, S5_GROUP)
    a_re, a_im = a_re.astype(f32), a_im.astype(f32)
    b_re, b_im = b_re.astype(f32), b_im.astype(f32)
    step = jnp.exp(log_step.astype(f32))[:, None]
    dt_re, dt_im = step * a_re, step * a_im
    mag = jnp.exp(dt_re)
    ab_re, ab_im = mag * jnp.cos(dt_im), mag * jnp.sin(dt_im)
    z_re, z_im = ab_re - 1.0, ab_im
    den = a_re * a_re + a_im * a_im
    f_re = (z_re * a_re + z_im * a_im) / den
    f_im = (z_im * a_re - z_re * a_im) / den
    bb_re = f_re[..., None] * b_re - f_im[..., None] * b_im
    bb_im = f_re[..., None] * b_im + f_im[..., None] * b_re
    bu_re = jnp.einsum('bsgc,gpc->bsgp', uf, bb_re)
    bu_im = jnp.einsum('bsgc,gpc->bsgp', uf, bb_im)
    shape = bu_re.shape
    elems = (jnp.broadcast_to(ab_re, shape), jnp.broadcast_to(ab_im, shape), bu_re, bu_im)
    _, _, x_re, x_im = lax.associative_scan(_complex_combine, elems, axis=1)
    y = (jnp.einsum('bsgp,gcp->bsgc', x_re, c_re.astype(f32))
         - jnp.einsum('bsgp,gcp->bsgc', x_im, c_im.astype(f32))
         + d.astype(f32) * uf)
    y = jax.nn.gelu(y.reshape(bsz, s, S5_WIDTH))
    return y * jax.nn.sigmoid(y @ w_glu.astype(f32) + b_glu.astype(f32))


def _hybrid_mixer(h, w_in, lru_conv_w, lru_conv_b, lru_wr, lru_br, lru_wi, lru_bi, lru_lambda,
                  s5_a_re, s5_a_im, s5_b_re, s5_b_im, s5_c_re, s5_c_im, s5_d, s5_log_step,
                  s5_w_glu, s5_b_glu, mix_norm_g, w_out):
    b, s, _ = h.shape
    proj = h @ w_in

    def heads(off):
        return proj[..., off:off + ATTN_WIDTH].reshape(b, s, N_ATTN_HEADS, HEAD_DIM)

    attn = _dilated_attention(heads(Q_OFF), heads(K_OFF), heads(V_OFF)).reshape(b, s, ATTN_WIDTH)
    lru = _rg_lru_branch(proj[..., LRU_X_OFF:LRU_X_OFF + LRU_WIDTH],
                         proj[..., LRU_G_OFF:LRU_G_OFF + LRU_WIDTH],
                         lru_conv_w, lru_conv_b, lru_wr, lru_br, lru_wi, lru_bi, lru_lambda)
    ssm = _s5_branch(proj[..., S5_OFF:S5_OFF + S5_WIDTH], s5_a_re, s5_a_im, s5_b_re, s5_b_im,
                     s5_c_re, s5_c_im, s5_d, s5_log_step, s5_w_glu, s5_b_glu)
    g_attn = mix_norm_g[:ATTN_WIDTH]
    g_lru = mix_norm_g[ATTN_WIDTH:ATTN_WIDTH + LRU_WIDTH]
    g_s5 = mix_norm_g[ATTN_WIDTH + LRU_WIDTH:]
    mixed = jnp.concatenate([_rms_norm(attn, g_attn), _rms_norm(lru, g_lru), _rms_norm(ssm, g_s5)],
                            axis=-1).astype(h.dtype)
    return mixed @ w_out


def _conv_ffn(h, w_up, conv_w, conv_b, w_down):
    up = _causal_dwconv(h @ w_up, conv_w, conv_b)
    gate, val = jnp.split(up, 2, axis=-1)
    return (jax.nn.gelu(gate) * val) @ w_down


def setup_inputs(seed: int = 0) -> dict:
    key = jax.random.key(seed)
    ks = iter(jax.random.split(key, 32))
    f32 = jnp.float32
    L = DEPTH

    def nrm(shape, scale):
        return jax.random.normal(next(ks), shape, f32) * scale

    x = nrm((BATCH, SEQ, D_MODEL), 1.0)
    w_in = nrm((L, D_MODEL, D_IN), D_MODEL ** -0.5)
    lru_conv_w = nrm((L, LRU_CONV, LRU_WIDTH), LRU_CONV ** -0.5)
    lru_conv_b = nrm((L, LRU_WIDTH), 0.02)
    lru_wr = nrm((L, N_LRU_HEADS, LRU_HEAD, LRU_HEAD), LRU_HEAD ** -0.5)
    lru_br = nrm((L, LRU_WIDTH), 0.02)
    lru_wi = nrm((L, N_LRU_HEADS, LRU_HEAD, LRU_HEAD), LRU_HEAD ** -0.5)
    lru_bi = nrm((L, LRU_WIDTH), 0.02)
    a_c = jax.random.uniform(next(ks), (L, LRU_WIDTH), f32, 0.9, 0.999)
    a0 = a_c ** (1.0 / LRU_C)
    lru_lambda = jnp.log(a0) - jnp.log1p(-a0)
    s5_a_re = -0.5 + nrm((L, N_S5_GROUPS, S5_STATE), 0.01)
    s5_a_im = jnp.pi * jnp.arange(S5_STATE, dtype=f32) + nrm((L, N_S5_GROUPS, S5_STATE), 0.01)
    s5_b_re = nrm((L, N_S5_GROUPS, S5_STATE, S5_GROUP), (2 * S5_GROUP) ** -0.5)
    s5_b_im = nrm((L, N_S5_GROUPS, S5_STATE, S5_GROUP), (2 * S5_GROUP) ** -0.5)
    s5_c_re = nrm((L, N_S5_GROUPS, S5_GROUP, S5_STATE), (2 * S5_STATE) ** -0.5)
    s5_c_im = nrm((L, N_S5_GROUPS, S5_GROUP, S5_STATE), (2 * S5_STATE) ** -0.5)
    s5_d = nrm((L, N_S5_GROUPS, S5_GROUP), 1.0)
    s5_log_step = jax.random.uniform(next(ks), (L, N_S5_GROUPS), f32, math.log(1e-3), math.log(1e-1))
    s5_w_glu = nrm((L, S5_WIDTH, S5_WIDTH), S5_WIDTH ** -0.5)
    s5_b_glu = nrm((L, S5_WIDTH), 0.02)
    mix_norm_g = 1.0 + nrm((L, D_MIX), 0.02)
    w_out = nrm((L, D_MIX, D_MODEL), D_MIX ** -0.5 * DEEPNORM_BETA)
    ln1_g = 1.0 + nrm((L, D_MODEL), 0.02)
    ln1_b = nrm((L, D_MODEL), 0.02)
    w_up = nrm((L, D_MODEL, 2 * D_FF), D_MODEL ** -0.5)
    ffn_conv_w = nrm((L, FFN_CONV, 2 * D_FF), FFN_CONV ** -0.5)
    ffn_conv_b = nrm((L, 2 * D_FF), 0.02)
    w_down = nrm((L, D_FF, D_MODEL), D_FF ** -0.5 * DEEPNORM_BETA)
    ln2_g = 1.0 + nrm((L, D_MODEL), 0.02)
    ln2_b = nrm((L, D_MODEL), 0.02)
    return {'x': x, 'w_in': w_in, 'lru_conv_w': lru_conv_w, 'lru_conv_b': lru_conv_b,
            'lru_wr': lru_wr, 'lru_br': lru_br, 'lru_wi': lru_wi, 'lru_bi': lru_bi,
            'lru_lambda': lru_lambda, 's5_a_re': s5_a_re, 's5_a_im': s5_a_im,
            's5_b_re': s5_b_re, 's5_b_im': s5_b_im, 's5_c_re': s5_c_re, 's5_c_im': s5_c_im,
            's5_d': s5_d, 's5_log_step': s5_log_step, 's5_w_glu': s5_w_glu, 's5_b_glu': s5_b_glu,
            'mix_norm_g': mix_norm_g, 'w_out': w_out, 'ln1_g': ln1_g, 'ln1_b': ln1_b,
            'w_up': w_up, 'ffn_conv_w': ffn_conv_w, 'ffn_conv_b': ffn_conv_b, 'w_down': w_down,
            'ln2_g': ln2_g, 'ln2_b': ln2_b}


def reference(x, w_in, lru_conv_w, lru_conv_b, lru_wr, lru_br, lru_wi, lru_bi, lru_lambda,
              s5_a_re, s5_a_im, s5_b_re, s5_b_im, s5_c_re, s5_c_im, s5_d, s5_log_step,
              s5_w_glu, s5_b_glu, mix_norm_g, w_out, ln1_g, ln1_b, w_up, ffn_conv_w, ffn_conv_b,
              w_down, ln2_g, ln2_b):
    h = x
    for l in range(DEPTH):
        mix = _hybrid_mixer(h, w_in[l], lru_conv_w[l], lru_conv_b[l], lru_wr[l], lru_br[l],
                            lru_wi[l], lru_bi[l], lru_lambda[l], s5_a_re[l], s5_a_im[l],
                            s5_b_re[l], s5_b_im[l], s5_c_re[l], s5_c_im[l], s5_d[l],
                            s5_log_step[l], s5_w_glu[l], s5_b_glu[l], mix_norm_g[l], w_out[l])
        h = _layer_norm(DEEPNORM_ALPHA * h + mix, ln1_g[l], ln1_b[l])
        ffn = _conv_ffn(h, w_up[l], ffn_conv_w[l], ffn_conv_b[l], w_down[l])
        h = _layer_norm(DEEPNORM_ALPHA * h + ffn, ln2_g[l], ln2_b[l])
    return h
```

```python
import functools

import jax
import jax.numpy as jnp
from jax import lax
from jax.experimental import pallas as pl
from jax.experimental.pallas import tpu as pltpu

F32 = jnp.float32
BF16 = jnp.bfloat16

HEAD_DIM = 64
HEAD_PAIR = 2 * HEAD_DIM
DILATED_PAIRS = ((128, 1), (512, 4), (2048, 16))
ATTN_BLOCK = 128
ROPE_THETA = 10000.0
LRU_HEAD = 64
LRU_CONV = 4
LRU_C = 8.0
S5_GROUP = 16
S5_STATE = 64
FFN_CONV = 3
DEPTH = 2
DEEPNORM_ALPHA = (2 * DEPTH) ** 0.25
LN_EPS = 1e-5
RMS_EPS = 1e-6
NEG_BIG = -1e30

LANES = 128
SUBLANES = 8
VMEM_LIMIT = 48 * 1024 * 1024


def _params(*sem):
    return pltpu.CompilerParams(dimension_semantics=sem, vmem_limit_bytes=VMEM_LIMIT)


def _gelu(x):
    return jax.nn.gelu(x)


def _sigmoid(x):
    return 1.0 / (1.0 + jnp.exp(-x))


def _shift_rows(x, s):
    return pltpu.roll(x, s, 0)


def _proj_kernel(h_ref, w_ref, cos_ref, sin_ref, qkv_ref, rest_ref, *, attn_width):
    hb = h_ref[...].astype(BF16)
    acc = jnp.dot(hb, w_ref[...], preferred_element_type=F32)
    cos = cos_ref[...]
    sin = sin_ref[...]
    lane = lax.broadcasted_iota(jnp.int32, cos.shape, 1)
    first_half = (lane & (HEAD_DIM // 2)) == 0
    n_rot = 2 * attn_width // LANES
    for cb in range(n_rot):
        x = acc[:, cb * LANES:(cb + 1) * LANES]
        partner = jnp.where(first_half, pltpu.roll(x, LANES - HEAD_DIM // 2, 1),
                            pltpu.roll(x, HEAD_DIM // 2, 1))
        r = x * cos + partner * sin
        if cb < n_rot // 2:
            r = r * (HEAD_DIM ** -0.5)
        qkv_ref[:, cb * LANES:(cb + 1) * LANES] = r.astype(BF16)
    qkv_ref[:, 2 * attn_width:3 * attn_width] = acc[:, 2 * attn_width:3 * attn_width].astype(BF16)
    rest_ref[...] = acc[:, 3 * attn_width:]


def _rope_tables(seq):
    half = HEAD_DIM // 2
    pos = jnp.arange(seq, dtype=F32)
    inv = ROPE_THETA ** (-jnp.arange(half, dtype=F32) * 2.0 / HEAD_DIM)
    ang = pos[:, None] * inv[None, :]
    cos = jnp.cos(ang)
    sin = jnp.sin(ang)
    cos_t = jnp.concatenate([cos, cos, cos, cos], axis=1)
    sin_t = jnp.concatenate([-sin, sin, -sin, sin], axis=1)
    return cos_t, sin_t


def _proj(h, w_in_bf, cos_t, sin_t, attn_width, tm):
    t, d = h.shape
    d_in = w_in_bf.shape[1]
    seq = cos_t.shape[0]
    nseq = seq // tm
    rest_w = d_in - 3 * attn_width
    return pl.pallas_call(
        functools.partial(_proj_kernel, attn_width=attn_width),
        grid=(t // tm,),
        in_specs=[
            pl.BlockSpec((tm, d), lambda i: (i, 0)),
            pl.BlockSpec((d, d_in), lambda i: (0, 0)),
            pl.BlockSpec((tm, LANES), lambda i: (i % nseq, 0)),
            pl.BlockSpec((tm, LANES), lambda i: (i % nseq, 0)),
        ],
        out_specs=[
            pl.BlockSpec((tm, 3 * attn_width), lambda i: (i, 0)),
            pl.BlockSpec((tm, rest_w), lambda i: (i, 0)),
        ],
        out_shape=[
            jax.ShapeDtypeStruct((t, 3 * attn_width), BF16),
            jax.ShapeDtypeStruct((t, rest_w), F32),
        ],
        compiler_params=_params("parallel"),
        name="proj_rope",
    )(h, w_in_bf, cos_t, sin_t)


def _attn_kernel(q_ref, ko_ref, vo_ref, kp_ref, vp_ref, o_ref, l_ref, *, nsub):
    n = pl.program_id(3)
    blk = (ATTN_BLOCK, ATTN_BLOCK)
    row = lax.broadcasted_iota(jnp.int32, blk, 0)
    col = lax.broadcasted_iota(jnp.int32, blk, 1)
    mask_own = col <= row
    mask_prev = col >= row
    head0 = col < HEAD_DIM
    prev_bias = jnp.where(n > 0, 0.0, NEG_BIG).astype(F32)
    contract_last = (((1,), (1,)), ((), ()))
    for j in range(nsub):
        rows = slice(j * ATTN_BLOCK, (j + 1) * ATTN_BLOCK)
        q = q_ref[rows, :]
        k_own = ko_ref[rows, :]
        v_own = vo_ref[rows, :]
        if j == 0:
            k_prev = kp_ref[...]
            v_prev = vp_ref[...]
        else:
            prows = slice((j - 1) * ATTN_BLOCK, j * ATTN_BLOCK)
            k_prev = ko_ref[prows, :]
            v_prev = vo_ref[prows, :]
        outs = []
        lses = []
        for hd in range(2):
            sel = head0 if hd == 0 else jnp.logical_not(head0)
            qh = jnp.where(sel, q, jnp.zeros_like(q))
            s_own = lax.dot_general(qh, k_own, contract_last, preferred_element_type=F32)
            s_prev = lax.dot_general(qh, k_prev, contract_last, preferred_element_type=F32)
            if j == 0:
                s_prev = s_prev + prev_bias
            s_own = jnp.where(mask_own, s_own, NEG_BIG)
            s_prev = jnp.where(mask_prev, s_prev, NEG_BIG)
            m = jnp.maximum(jnp.max(s_own, axis=1, keepdims=True),
                            jnp.max(s_prev, axis=1, keepdims=True))
            p_own = jnp.exp(s_own - m)
            p_prev = jnp.exp(s_prev - m)
            denom = jnp.sum(p_own, axis=1, keepdims=True) + jnp.sum(p_prev, axis=1, keepdims=True)
            pv = (jnp.dot(p_own.astype(BF16), v_own, preferred_element_type=F32)
                  + jnp.dot(p_prev.astype(BF16), v_prev, preferred_element_type=F32))
            outs.append(pv / denom)
            lses.append(m + jnp.log(denom))
        o_ref[rows, :] = jnp.where(head0, outs[0], outs[1])
        l_ref[rows, :] = jnp.where(head0, lses[0], lses[1])


def _attn_branch(qkv, batch, seq, attn_width, dilation):
    m = seq // dilation
    qb = min(512, m)
    nsub = qb // ATTN_BLOCK
    nblk = m // qb
    npair = attn_width // HEAD_PAIR
    cols = 3 * npair
    qkv_v = qkv.reshape(batch, m, dilation * 3 * attn_width)

    def own(off):
        return pl.BlockSpec((None, qb, HEAD_PAIR),
                            lambda b, c, p, n: (b, n, c * cols + off * npair + p))

    def prev(off):
        return pl.BlockSpec((None, ATTN_BLOCK, HEAD_PAIR),
                            lambda b, c, p, n: (b, jnp.maximum(n * nsub - 1, 0), c * cols + off * npair + p))

    out_spec = pl.BlockSpec((None, qb, HEAD_PAIR), lambda b, c, p, n: (b, n, c * npair + p))
    out_sds = jax.ShapeDtypeStruct((batch, m, dilation * attn_width), F32)
    o, lse = pl.pallas_call(
        functools.partial(_attn_kernel, nsub=nsub),
        grid=(batch, dilation, npair, nblk),
        in_specs=[own(0), own(1), own(2), prev(1), prev(2)],
        out_specs=[out_spec, out_spec],
        out_shape=[out_sds, out_sds],
        compiler_params=_params("parallel", "parallel", "parallel", "arbitrary"),
        name=f"dilated_attn_d{dilation}",
    )(qkv_v, qkv_v, qkv_v, qkv_v, qkv_v)
    t = batch * seq
    return o.reshape(t, attn_width), lse.reshape(t, attn_width)


def _lru_kernel(x_ref, xp_ref, g_ref, cw_ref, cb_ref, w_ref, b_ref, lam_ref, ng_ref,
                o_ref, carry_ref, *, width):
    n = pl.program_id(1)

    @pl.when(n == 0)
    def _():
        carry_ref[...] = jnp.zeros_like(carry_ref)

    x = x_ref[...]
    rows = x.shape[0]
    xp = xp_ref[...] * jnp.where(n > 0, 1.0, 0.0).astype(F32)
    xe = jnp.concatenate([xp, x], axis=0)
    cw = cw_ref[...]
    xc = cw[LRU_CONV - 1:LRU_CONV, :] * x + cb_ref[...]
    for k in range(1, LRU_CONV):
        tap = LRU_CONV - 1 - k
        xc = xc + cw[tap:tap + 1, :] * _shift_rows(xe, k)[SUBLANES:, :]
    gates = jnp.dot(xc.astype(BF16), w_ref[...], preferred_element_type=F32) + b_ref[...]
    r = _sigmoid(gates[:, :width])
    i = _sigmoid(gates[:, width:])
    lam = lam_ref[...]
    softplus_neg = jnp.maximum(-lam, 0.0) + jnp.log1p(jnp.exp(-jnp.abs(lam)))
    log_a = (-LRU_C) * r * softplus_neg
    a = jnp.exp(log_a)
    t = jnp.tanh(log_a)
    u = jnp.sqrt(-2.0 * t / (1.0 - t)) * (i * xc)
    row = lax.broadcasted_iota(jnp.int32, x.shape, 0)
    s = 1
    while s < rows:
        live = row >= s
        a_sh = jnp.where(live, _shift_rows(a, s), 1.0)
        u_sh = jnp.where(live, _shift_rows(u, s), 0.0)
        u = a * u_sh + u
        a = a * a_sh
        s *= 2
    h = u + a * carry_ref[...]
    carry_ref[...] = h[rows - 1:rows, :]
    out = h * _gelu(g_ref[...])
    ms = jnp.mean(out * out, axis=-1, keepdims=True)
    o_ref[...] = (out * lax.rsqrt(ms + RMS_EPS) * ng_ref[...]).astype(BF16)


def _lru(rest, batch, seq, width, conv_w, conv_b, w_gate_bf, b_gate, lam, norm_g, chunk):
    rest_v = rest.reshape(batch, seq, rest.shape[-1])
    nchunk = seq // chunk
    per8 = chunk // SUBLANES

    def full(shape):
        return pl.BlockSpec(shape, lambda b, n: (0,) * len(shape))

    out = pl.pallas_call(
        functools.partial(_lru_kernel, width=width),
        grid=(batch, nchunk),
        in_specs=[
            pl.BlockSpec((None, chunk, width), lambda b, n: (b, n, 0)),
            pl.BlockSpec((None, SUBLANES, width), lambda b, n: (b, jnp.maximum(n * per8 - 1, 0), 0)),
            pl.BlockSpec((None, chunk, width), lambda b, n: (b, n, 1)),
            full((LRU_CONV, width)),
            full((1, width)),
            full((width, 2 * width)),
            full((1, 2 * width)),
            full((1, width)),
            full((1, width)),
        ],
        out_specs=pl.BlockSpec((None, chunk, width), lambda b, n: (b, n, 0)),
        out_shape=jax.ShapeDtypeStruct((batch, seq, width), BF16),
        scratch_shapes=[pltpu.VMEM((1, width), F32)],
        compiler_params=_params("parallel", "arbitrary"),
        name="rg_lru",
    )(rest_v, rest_v, rest_v, conv_w, conv_b.reshape(1, width), w_gate_bf,
      b_gate.reshape(1, 2 * width), lam.reshape(1, width), norm_g.reshape(1, width))
    return out.reshape(batch * seq, width)


def _block_diag(w):
    h, i, j = w.shape
    eye = jnp.eye(h, dtype=w.dtype)
    return jnp.einsum('hij,hg->higj', w, eye).reshape(h * i, h * j)


def _s5_kernel(u_ref, bbd_ref, pwr_ref, pwi_ref, cbd_ref, d_ref, wg_ref, bg_ref, ng_ref,
               o_ref, bu_ref, xb_ref, cr_ref, ci_ref, *, nstate):
    n = pl.program_id(1)

    @pl.when(n == 0)
    def _():
        cr_ref[...] = jnp.zeros_like(cr_ref)
        ci_ref[...] = jnp.zeros_like(ci_ref)

    u = u_ref[...]
    rows = u.shape[0]
    bu_ref[...] = jnp.dot(u.astype(BF16), bbd_ref[...], preferred_element_type=F32)
    row = lax.broadcasted_iota(jnp.int32, (rows, LANES), 0)
    for j in range(nstate // LANES):
        re_cols = slice(j * LANES, (j + 1) * LANES)
        im_cols = slice(nstate + j * LANES, nstate + (j + 1) * LANES)
        xr = bu_ref[:, re_cols]
        xi = bu_ref[:, im_cols]
        s = 1
        while s < rows:
            lr = pwr_ref[s - 1:s, re_cols]
            li = pwi_ref[s - 1:s, re_cols]
            live = row >= s
            sr = jnp.where(live, _shift_rows(xr, s), 0.0)
            si = jnp.where(live, _shift_rows(xi, s), 0.0)
            xr, xi = xr + (lr * sr - li * si), xi + (lr * si + li * sr)
            s *= 2
        pr = pwr_ref[:, re_cols]
        pi = pwi_ref[:, re_cols]
        cr = cr_ref[:, re_cols]
        ci = ci_ref[:, re_cols]
        xr, xi = xr + (pr * cr - pi * ci), xi + (pr * ci + pi * cr)
        cr_ref[:, re_cols] = xr[rows - 1:rows, :]
        ci_ref[:, re_cols] = xi[rows - 1:rows, :]
        xb_ref[:, re_cols] = xr.astype(BF16)
        xb_ref[:, im_cols] = xi.astype(BF16)
    y = jnp.dot(xb_ref[...], cbd_ref[...], preferred_element_type=F32) + d_ref[...] * u
    y = _gelu(y)
    z = jnp.dot(y.astype(BF16), wg_ref[...], preferred_element_type=F32) + bg_ref[...]
    out = y * _sigmoid(z)
    ms = jnp.mean(out * out, axis=-1, keepdims=True)
    o_ref[...] = (out * lax.rsqrt(ms + RMS_EPS) * ng_ref[...]).astype(BF16)


def _s5_tables(a_re, a_im, b_re, b_im, c_re, c_im, log_step, chunk):
    g, p = a_re.shape
    step = jnp.exp(log_step)[:, None]
    dt_re, dt_im = step * a_re, step * a_im
    mag = jnp.exp(dt_re)
    ab_re, ab_im = mag * jnp.cos(dt_im), mag * jnp.sin(dt_im)
    z_re, z_im = ab_re - 1.0, ab_im
    den = a_re * a_re + a_im * a_im
    f_re = (z_re * a_re + z_im * a_im) / den
    f_im = (z_im * a_re - z_re * a_im) / den
    bb_re = f_re[..., None] * b_re - f_im[..., None] * b_im
    bb_im = f_re[..., None] * b_im + f_im[..., None] * b_re
    eye = jnp.eye(g, dtype=F32)

    def in_map(bb):
        c = bb.shape[-1]
        return jnp.einsum('gpc,gh->gchp', bb, eye).reshape(g * c, g * p)

    def out_map(cc):
        c = cc.shape[1]
        return jnp.einsum('gcp,gh->gphc', cc, eye).reshape(g * p, g * c)

    bbd = jnp.concatenate([in_map(bb_re), in_map(bb_im)], axis=1)
    cbd = jnp.concatenate([out_map(c_re), out_map(-c_im)], axis=0)
    pr = ab_re.reshape(1, g * p)
    pi = ab_im.reshape(1, g * p)
    s = 1
    while s < chunk:
        lr, li = pr[s - 1:s], pi[s - 1:s]
        pr, pi = (jnp.concatenate([pr, pr * lr - pi * li], axis=0),
                  jnp.concatenate([pi, pr * li + pi * lr], axis=0))
        s *= 2
    return bbd.astype(BF16), pr, pi, cbd.astype(BF16)


def _s5(rest, batch, seq, col_block, width, tables, d, w_glu_bf, b_glu, norm_g, chunk):
    bbd, pwr, pwi, cbd = tables
    nstate = pwr.shape[1]
    rest_v = rest.reshape(batch, seq, rest.shape[-1])

    def full(shape):
        return pl.BlockSpec(shape, lambda b, n: (0,) * len(shape))

    out = pl.pallas_call(
        functools.partial(_s5_kernel, nstate=nstate),
        grid=(batch, seq // chunk),
        in_specs=[
            pl.BlockSpec((None, chunk, width), lambda b, n: (b, n, col_block)),
            full((width, 2 * nstate)),
            full((chunk, nstate)),
            full((chunk, nstate)),
            full((2 * nstate, width)),
            full((1, width)),
            full((width, width)),
            full((1, width)),
            full((1, width)),
        ],
        out_specs=pl.BlockSpec((None, chunk, width), lambda b, n: (b, n, 0)),
        out_shape=jax.ShapeDtypeStruct((batch, seq, width), BF16),
        scratch_shapes=[
            pltpu.VMEM((chunk, 2 * nstate), F32),
            pltpu.VMEM((chunk, 2 * nstate), BF16),
            pltpu.VMEM((1, nstate), F32),
            pltpu.VMEM((1, nstate), F32),
        ],
        compiler_params=_params("parallel", "arbitrary"),
        name="s5_scan",
    )(rest_v, bbd, pwr, pwi, cbd, d.reshape(1, width), w_glu_bf, b_glu.reshape(1, width),
      norm_g.reshape(1, width))
    return out.reshape(batch * seq, width)


def _layer_norm_rows(y, g, b):
    mu = jnp.mean(y, axis=-1, keepdims=True)
    yc = y - mu
    var = jnp.mean(yc * yc, axis=-1, keepdims=True)
    return yc * lax.rsqrt(var + LN_EPS) * g + b


def _mix_kernel(o1_ref, o2_ref, o3_ref, l1_ref, l2_ref, l3_ref, lru_ref, s5_ref, h_ref,
                w_ref, ga_ref, lg_ref, lb_ref, out_ref):
    l1, l2, l3 = l1_ref[...], l2_ref[...], l3_ref[...]
    m = jnp.maximum(jnp.maximum(l1, l2), l3)
    e1, e2, e3 = jnp.exp(l1 - m), jnp.exp(l2 - m), jnp.exp(l3 - m)
    attn = (e1 * o1_ref[...] + e2 * o2_ref[...] + e3 * o3_ref[...]) / (e1 + e2 + e3)
    ms = jnp.mean(attn * attn, axis=-1, keepdims=True)
    attn_n = (attn * lax.rsqrt(ms + RMS_EPS) * ga_ref[...]).astype(BF16)
    mixed = jnp.concatenate([attn_n, lru_ref[...], s5_ref[...]], axis=1)
    mix = jnp.dot(mixed, w_ref[...], preferred_element_type=F32)
    y = DEEPNORM_ALPHA * h_ref[...] + mix
    out_ref[...] = _layer_norm_rows(y, lg_ref[...], lb_ref[...])


def _mix(attn_outs, lru, s5, h, w_out_bf, g_attn, ln_g, ln_b, tm):
    t, d = h.shape
    aw = lru.shape[1]
    sw = s5.shape[1]
    (o1, l1), (o2, l2), (o3, l3) = attn_outs

    def rows(w):
        return pl.BlockSpec((tm, w), lambda i: (i, 0))

    def full(shape):
        return pl.BlockSpec(shape, lambda i: (0,) * len(shape))

    return pl.pallas_call(
        _mix_kernel,
        grid=(t // tm,),
        in_specs=[rows(aw)] * 6 + [rows(aw), rows(sw), rows(d), full((d, d)),
                                   full((1, aw)), full((1, d)), full((1, d))],
        out_specs=rows(d),
        out_shape=jax.ShapeDtypeStruct((t, d), F32),
        compiler_params=_params("parallel"),
        name="mix_out_ln",
    )(o1, o2, o3, l1, l2, l3, lru, s5, h, w_out_bf, g_attn.reshape(1, aw),
      ln_g.reshape(1, d), ln_b.reshape(1, d))


def _ffn_kernel(h_ref, hp_ref, wg_ref, wv_ref, cwg_ref, cwv_ref, cbg_ref, cbv_ref, wd_ref,
                lg_ref, lb_ref, out_ref, acc_ref, *, tiles_per_seq):
    i = pl.program_id(0)
    j = pl.program_id(1)
    keep_prev = jnp.where(i % tiles_per_seq == 0, 0.0, 1.0).astype(F32)
    he = jnp.concatenate([hp_ref[...] * keep_prev, h_ref[...]], axis=0).astype(BF16)

    def conv_branch(w_ref, cw_ref, cb_ref):
        up = jnp.dot(he, w_ref[...], preferred_element_type=F32)
        cw = cw_ref[...]
        y = cw[FFN_CONV - 1:FFN_CONV, :] * up
        for k in range(1, FFN_CONV):
            tap = FFN_CONV - 1 - k
            y = y + cw[tap:tap + 1, :] * _shift_rows(up, k)
        return y[SUBLANES:, :] + cb_ref[...]

    gate = conv_branch(wg_ref, cwg_ref, cbg_ref)
    val = conv_branch(wv_ref, cwv_ref, cbv_ref)
    act = (_gelu(gate) * val).astype(BF16)
    part = jnp.dot(act, wd_ref[...], preferred_element_type=F32)

    @pl.when(j == 0)
    def _():
        acc_ref[...] = part

    @pl.when(j > 0)
    def _():
        acc_ref[...] += part

    @pl.when(j == pl.num_programs(1) - 1)
    def _():
        y = DEEPNORM_ALPHA * h_ref[...] + acc_ref[...]
        out_ref[...] = _layer_norm_rows(y, lg_ref[...], lb_ref[...])


def _ffn(h, seq, w_up_bf, conv_w, conv_b, w_down_bf, ln_g, ln_b, tm, cw):
    t, d = h.shape
    d_ff = w_down_bf.shape[0]
    nff = d_ff // cw
    per8 = tm // SUBLANES
    conv_b2 = conv_b.reshape(1, 2 * d_ff)
    return pl.pallas_call(
        functools.partial(_ffn_kernel, tiles_per_seq=seq // tm),
        grid=(t // tm, nff),
        in_specs=[
            pl.BlockSpec((tm, d), lambda i, j: (i, 0)),
            pl.BlockSpec((SUBLANES, d), lambda i, j: (jnp.maximum(i * per8 - 1, 0), 0)),
            pl.BlockSpec((d, cw), lambda i, j: (0, j)),
            pl.BlockSpec((d, cw), lambda i, j: (0, nff + j)),
            pl.BlockSpec((FFN_CONV, cw), lambda i, j: (0, j)),
            pl.BlockSpec((FFN_CONV, cw), lambda i, j: (0, nff + j)),
            pl.BlockSpec((1, cw), lambda i, j: (0, j)),
            pl.BlockSpec((1, cw), lambda i, j: (0, nff + j)),
            pl.BlockSpec((cw, d), lambda i, j: (j, 0)),
            pl.BlockSpec((1, d), lambda i, j: (0, 0)),
            pl.BlockSpec((1, d), lambda i, j: (0, 0)),
        ],
        out_specs=pl.BlockSpec((tm, d), lambda i, j: (i, 0)),
        out_shape=jax.ShapeDtypeStruct((t, d), F32),
        scratch_shapes=[pltpu.VMEM((tm, d), F32)],
        compiler_params=_params("parallel", "arbitrary"),
        name="conv_ffn_ln",
    )(h, h, w_up_bf, w_up_bf, conv_w, conv_w, conv_b2, conv_b2, w_down_bf,
      ln_g.reshape(1, d), ln_b.reshape(1, d))


def kernel(x, w_in, lru_conv_w, lru_conv_b, lru_wr, lru_br, lru_wi, lru_bi, lru_lambda, s5_a_re, s5_a_im, s5_b_re, s5_b_im, s5_c_re, s5_c_im, s5_d, s5_log_step, s5_w_glu, s5_b_glu, mix_norm_g, w_out, ln1_g, ln1_b, w_up, ffn_conv_w, ffn_conv_b, w_down, ln2_g, ln2_b):
    batch, seq, d_model = x.shape
    depth = w_in.shape[0]
    lru_width = lru_conv_w.shape[-1]
    s5_width = s5_w_glu.shape[-1]
    attn_width = (w_in.shape[-1] - 2 * lru_width - s5_width) // 3
    assert attn_width % HEAD_PAIR == 0 and lru_width == attn_width
    assert all(w // d == ATTN_BLOCK for w, d in DILATED_PAIRS)
    s5_chunk = 128
    lru_chunk = 256
    row_tile = 512
    ff_chunk = 512
    s5_col_block = (2 * lru_width) // s5_width
    assert s5_col_block * s5_width == 2 * lru_width

    cos_t, sin_t = _rope_tables(seq)
    h = x.reshape(batch * seq, d_model)
    for l in range(depth):
        qkv, rest = _proj(h, w_in[l].astype(BF16), cos_t, sin_t, attn_width, row_tile)
        attn_outs = [_attn_branch(qkv, batch, seq, attn_width, d) for _, d in DILATED_PAIRS]
        w_gate = jnp.concatenate([_block_diag(lru_wr[l]), _block_diag(lru_wi[l])], axis=1).astype(BF16)
        b_gate = jnp.concatenate([lru_br[l], lru_bi[l]])
        lru = _lru(rest, batch, seq, lru_width, lru_conv_w[l], lru_conv_b[l], w_gate, b_gate,
                   lru_lambda[l], mix_norm_g[l, attn_width:attn_width + lru_width], lru_chunk)
        tables = _s5_tables(s5_a_re[l], s5_a_im[l], s5_b_re[l], s5_b_im[l], s5_c_re[l], s5_c_im[l],
                            s5_log_step[l], s5_chunk)
        s5 = _s5(rest, batch, seq, s5_col_block, s5_width, tables, s5_d[l].reshape(-1),
                 s5_w_glu[l].astype(BF16), s5_b_glu[l], mix_norm_g[l, attn_width + lru_width:], s5_chunk)
        h1 = _mix(attn_outs, lru, s5, h, w_out[l].astype(BF16), mix_norm_g[l, :attn_width],
                  ln1_g[l], ln1_b[l], row_tile)
        h = _ffn(h1, seq, w_up[l].astype(BF16), ffn_conv_w[l], ffn_conv_b[l], w_down[l].astype(BF16),
                 ln2_g[l], ln2_b[l], row_tile, ff_chunk)
    return h.reshape(batch, seq, d_model)
```

```python
import functools

import jax
import jax.numpy as jnp
from jax import lax
from jax.experimental import pallas as pl
from jax.experimental.pallas import tpu as pltpu

F32 = jnp.float32
BF16 = jnp.bfloat16

HEAD_DIM = 64
HEAD_PAIR = 2 * HEAD_DIM
DILATED_PAIRS = ((128, 1), (512, 4), (2048, 16))
ATTN_BLOCK = 128
ROPE_THETA = 10000.0
LRU_HEAD = 64
LRU_CONV = 4
LRU_C = 8.0
S5_GROUP = 16
S5_STATE = 64
FFN_CONV = 3
DEPTH = 2
DEEPNORM_ALPHA = (2 * DEPTH) ** 0.25
LN_EPS = 1e-5
RMS_EPS = 1e-6
NEG_BIG = -1e30

LANES = 128
SUBLANES = 8
VMEM_LIMIT = 48 * 1024 * 1024


def _params(*sem):
    return pltpu.CompilerParams(dimension_semantics=sem, vmem_limit_bytes=VMEM_LIMIT)


def _gelu(x):
    return jax.nn.gelu(x)


def _sigmoid(x):
    return 1.0 / (1.0 + jnp.exp(-x))


def _shift_rows(x, s):
    return pltpu.roll(x, s, 0)


def _proj_kernel(h_ref, w_ref, cos_ref, sin_ref, *refs, attn_width, dilations):
    out_refs = refs[:len(dilations)]
    rest_ref, stage_ref = refs[len(dilations):]
    hb = h_ref[...].astype(BF16)
    acc = jnp.dot(hb, w_ref[...], preferred_element_type=F32)
    rows = acc.shape[0]
    cos = cos_ref[...]
    sin = sin_ref[...]
    lane = lax.broadcasted_iota(jnp.int32, cos.shape, 1)
    first_half = (lane & (HEAD_DIM // 2)) == 0
    n_slab = 3 * attn_width // LANES
    n_rot = 2 * attn_width // LANES
    for cb in range(n_slab):
        x = acc[:, cb * LANES:(cb + 1) * LANES]
        if cb < n_rot:
            partner = jnp.where(first_half, pltpu.roll(x, LANES - HEAD_DIM // 2, 1),
                                pltpu.roll(x, HEAD_DIM // 2, 1))
            x = x * cos + partner * sin
        if cb < n_rot // 2:
            x = x * (HEAD_DIM ** -0.5)
        stage_ref[cb] = x
    rest_ref[...] = acc[:, 3 * attn_width:]
    for out_ref, d in zip(out_refs, dilations):
        for c in range(d):
            for cb in range(n_slab):
                if d == 1:
                    piece = stage_ref[cb]
                else:
                    piece = stage_ref[cb, pl.ds(c, rows // d, stride=d), :]
                out_ref[c, cb] = piece.astype(BF16)


def _rope_tables(seq):
    half = HEAD_DIM // 2
    pos = jnp.arange(seq, dtype=F32)
    inv = ROPE_THETA ** (-jnp.arange(half, dtype=F32) * 2.0 / HEAD_DIM)
    ang = pos[:, None] * inv[None, :]
    cos = jnp.cos(ang)
    sin = jnp.sin(ang)
    cos_t = jnp.concatenate([cos, cos, cos, cos], axis=1)
    sin_t = jnp.concatenate([-sin, sin, -sin, sin], axis=1)
    return cos_t, sin_t


def _proj(h, w_in_bf, cos_t, sin_t, batch, attn_width, dilations, tm):
    t, d = h.shape
    d_in = w_in_bf.shape[1]
    seq = cos_t.shape[0]
    nseq = seq // tm
    rest_w = d_in - 3 * attn_width
    n_slab = 3 * attn_width // LANES
    qkv_specs = [pl.BlockSpec((None, dil, n_slab, tm // dil, LANES),
                              lambda i: (i // nseq, 0, 0, i % nseq, 0)) for dil in dilations]
    qkv_shapes = [jax.ShapeDtypeStruct((batch, dil, n_slab, seq // dil, LANES), BF16) for dil in dilations]
    outs = pl.pallas_call(
        functools.partial(_proj_kernel, attn_width=attn_width, dilations=tuple(dilations)),
        grid=(t // tm,),
        in_specs=[
            pl.BlockSpec((tm, d), lambda i: (i, 0)),
            pl.BlockSpec((d, d_in), lambda i: (0, 0)),
            pl.BlockSpec((tm, LANES), lambda i: (i % nseq, 0)),
            pl.BlockSpec((tm, LANES), lambda i: (i % nseq, 0)),
        ],
        out_specs=qkv_specs + [pl.BlockSpec((tm, rest_w), lambda i: (i, 0))],
        out_shape=qkv_shapes + [jax.ShapeDtypeStruct((t, rest_w), F32)],
        scratch_shapes=[pltpu.VMEM((n_slab, tm, LANES), F32)],
        compiler_params=_params("parallel"),
        name="proj_rope",
    )(h, w_in_bf, cos_t, sin_t)
    return outs[:-1], outs[-1]


def _attn_kernel(q_ref, k_ref, v_ref, kp_ref, vp_ref, o_ref, l_ref, *, nsub):
    n = pl.program_id(3)
    win = (ATTN_BLOCK, 2 * ATTN_BLOCK)
    row = lax.broadcasted_iota(jnp.int32, win, 0)
    col = lax.broadcasted_iota(jnp.int32, win, 1)
    band_bias = jnp.where((col >= row) & (col <= row + ATTN_BLOCK), 0.0, NEG_BIG).astype(F32)
    no_prev = jnp.where(n > 0, 0.0, NEG_BIG).astype(F32)
    first_bias = band_bias + jnp.where(col < ATTN_BLOCK, no_prev, 0.0)
    head0_q = lax.broadcasted_iota(jnp.int32, (ATTN_BLOCK, HEAD_PAIR), 1) < HEAD_DIM
    head0_kv = lax.broadcasted_iota(jnp.int32, (2 * ATTN_BLOCK, HEAD_PAIR), 1) < HEAD_DIM
    contract_last = (((1,), (1,)), ((), ()))
    for j in range(nsub):
        rows = slice(j * ATTN_BLOCK, (j + 1) * ATTN_BLOCK)
        q = q_ref[rows, :]
        if j == 0:
            kw = jnp.concatenate([kp_ref[...], k_ref[rows, :]], axis=0)
            vw = jnp.concatenate([vp_ref[...], v_ref[rows, :]], axis=0)
            bias = first_bias
        else:
            wrows = slice((j - 1) * ATTN_BLOCK, (j + 1) * ATTN_BLOCK)
            kw = k_ref[wrows, :]
            vw = v_ref[wrows, :]
            bias = band_bias
        pvs = []
        maxes = []
        for hd in range(2):
            sel_q = head0_q if hd == 0 else jnp.logical_not(head0_q)
            sel_kv = head0_kv if hd == 0 else jnp.logical_not(head0_kv)
            qh = jnp.where(sel_q, q, jnp.zeros_like(q))
            s = lax.dot_general(qh, kw, contract_last, preferred_element_type=F32) + bias
            m = jnp.max(s, axis=1, keepdims=True)
            p = jnp.exp(s - m).astype(BF16)
            v_aug = jnp.where(sel_kv, vw, jnp.ones_like(vw))
            pvs.append(jnp.dot(p, v_aug, preferred_element_type=F32))
            maxes.append(m)
        num = jnp.where(head0_q, pvs[0], pvs[1])
        den = pltpu.roll(jnp.where(head0_q, pvs[1], pvs[0]), HEAD_DIM, 1)
        o_ref[rows, :] = num / den
        l_ref[rows, :] = jnp.where(head0_q, maxes[0], maxes[1]) + jnp.log(den)


def _attn_branch(qkv_cm, dilation):
    batch, _, n_slab, m, _ = qkv_cm.shape
    npair = n_slab // 3
    qb = min(512, m)
    nsub = qb // ATTN_BLOCK
    nblk = m // qb

    def own(off):
        return pl.BlockSpec((None, None, None, qb, HEAD_PAIR),
                            lambda b, c, p, n: (b, c, off * npair + p, n, 0))

    def prev(off):
        return pl.BlockSpec((None, None, None, ATTN_BLOCK, HEAD_PAIR),
                            lambda b, c, p, n: (b, c, off * npair + p, jnp.maximum(n * nsub - 1, 0), 0))

    out_spec = pl.BlockSpec((None, None, None, qb, HEAD_PAIR), lambda b, c, p, n: (b, c, p, n, 0))
    out_sds = jax.ShapeDtypeStruct((batch, dilation, npair, m, HEAD_PAIR), F32)
    return pl.pallas_call(
        functools.partial(_attn_kernel, nsub=nsub),
        grid=(batch, dilation, npair, nblk),
        in_specs=[own(0), own(1), own(2), prev(1), prev(2)],
        out_specs=[out_spec, out_spec],
        out_shape=[out_sds, out_sds],
        compiler_params=_params("parallel", "parallel", "parallel", "arbitrary"),
        name=f"dilated_attn_d{dilation}",
    )(qkv_cm, qkv_cm, qkv_cm, qkv_cm, qkv_cm)


def _lru_kernel(x_ref, xp_ref, g_ref, cw_ref, cb_ref, w_ref, b_ref, lam_ref, ng_ref,
                o_ref, carry_ref, *, width):
    n = pl.program_id(1)

    @pl.when(n == 0)
    def _():
        carry_ref[...] = jnp.zeros_like(carry_ref)

    x = x_ref[...]
    rows = x.shape[0]
    xp = xp_ref[...] * jnp.where(n > 0, 1.0, 0.0).astype(F32)
    xe = jnp.concatenate([xp, x], axis=0)
    cw = cw_ref[...]
    xc = cw[LRU_CONV - 1:LRU_CONV, :] * x + cb_ref[...]
    for k in range(1, LRU_CONV):
        tap = LRU_CONV - 1 - k
        xc = xc + cw[tap:tap + 1, :] * _shift_rows(xe, k)[SUBLANES:, :]
    gates = jnp.dot(xc.astype(BF16), w_ref[...], preferred_element_type=F32) + b_ref[...]
    r = _sigmoid(gates[:, :width])
    i = _sigmoid(gates[:, width:])
    lam = lam_ref[...]
    softplus_neg = jnp.maximum(-lam, 0.0) + jnp.log1p(jnp.exp(-jnp.abs(lam)))
    log_a = (-LRU_C) * r * softplus_neg
    a = jnp.exp(log_a)
    t = jnp.tanh(log_a)
    u = jnp.sqrt(-2.0 * t / (1.0 - t)) * (i * xc)
    row = lax.broadcasted_iota(jnp.int32, x.shape, 0)
    s = 1
    while s < rows:
        live = row >= s
        a_sh = jnp.where(live, _shift_rows(a, s), 1.0)
        u_sh = jnp.where(live, _shift_rows(u, s), 0.0)
        u = a * u_sh + u
        a = a * a_sh
        s *= 2
    h = u + a * carry_ref[...]
    carry_ref[...] = h[rows - 1:rows, :]
    out = h * _gelu(g_ref[...])
    ms = jnp.mean(out * out, axis=-1, keepdims=True)
    o_ref[...] = (out * lax.rsqrt(ms + RMS_EPS) * ng_ref[...]).astype(BF16)


def _lru(rest, batch, seq, width, conv_w, conv_b, w_gate_bf, b_gate, lam, norm_g, chunk):
    rest_v = rest.reshape(batch, seq, rest.shape[-1])
    nchunk = seq // chunk
    per8 = chunk // SUBLANES

    def full(shape):
        return pl.BlockSpec(shape, lambda b, n: (0,) * len(shape))

    out = pl.pallas_call(
        functools.partial(_lru_kernel, width=width),
        grid=(batch, nchunk),
        in_specs=[
            pl.BlockSpec((None, chunk, width), lambda b, n: (b, n, 0)),
            pl.BlockSpec((None, SUBLANES, width), lambda b, n: (b, jnp.maximum(n * per8 - 1, 0), 0)),
            pl.BlockSpec((None, chunk, width), lambda b, n: (b, n, 1)),
            full((LRU_CONV, width)),
            full((1, width)),
            full((width, 2 * width)),
            full((1, 2 * width)),
            full((1, width)),
            full((1, width)),
        ],
        out_specs=pl.BlockSpec((None, chunk, width), lambda b, n: (b, n, 0)),
        out_shape=jax.ShapeDtypeStruct((batch, seq, width), BF16),
        scratch_shapes=[pltpu.VMEM((1, width), F32)],
        compiler_params=_params("parallel", "arbitrary"),
        name="rg_lru",
    )(rest_v, rest_v, rest_v, conv_w, conv_b.reshape(1, width), w_gate_bf,
      b_gate.reshape(1, 2 * width), lam.reshape(1, width), norm_g.reshape(1, width))
    return out.reshape(batch * seq, width)


def _block_diag(w):
    h, i, j = w.shape
    eye = jnp.eye(h, dtype=w.dtype)
    return jnp.einsum('hij,hg->higj', w, eye).reshape(h * i, h * j)


def _s5_kernel(u_ref, bbd_ref, pwr_ref, pwi_ref, cbd_ref, d_ref, wg_ref, bg_ref, ng_ref,
               o_ref, bu_ref, xb_ref, cr_ref, ci_ref, *, nstate):
    n = pl.program_id(1)

    @pl.when(n == 0)
    def _():
        cr_ref[...] = jnp.zeros_like(cr_ref)
        ci_ref[...] = jnp.zeros_like(ci_ref)

    u = u_ref[...]
    rows = u.shape[0]
    bu_ref[...] = jnp.dot(u.astype(BF16), bbd_ref[...], preferred_element_type=F32)
    row = lax.broadcasted_iota(jnp.int32, (rows, LANES), 0)
    for j in range(nstate // LANES):
        re_cols = slice(j * LANES, (j + 1) * LANES)
        im_cols = slice(nstate + j * LANES, nstate + (j + 1) * LANES)
        xr = bu_ref[:, re_cols]
        xi = bu_ref[:, im_cols]
        s = 1
        while s < rows:
            lr = pwr_ref[s - 1:s, re_cols]
            li = pwi_ref[s - 1:s, re_cols]
            live = row >= s
            sr = jnp.where(live, _shift_rows(xr, s), 0.0)
            si = jnp.where(live, _shift_rows(xi, s), 0.0)
            xr, xi = xr + (lr * sr - li * si), xi + (lr * si + li * sr)
            s *= 2
        pr = pwr_ref[:, re_cols]
        pi = pwi_ref[:, re_cols]
        cr = cr_ref[:, re_cols]
        ci = ci_ref[:, re_cols]
        xr, xi = xr + (pr * cr - pi * ci), xi + (pr * ci + pi * cr)
        cr_ref[:, re_cols] = xr[rows - 1:rows, :]
        ci_ref[:, re_cols] = xi[rows - 1:rows, :]
        xb_ref[:, re_cols] = xr.astype(BF16)
        xb_ref[:, im_cols] = xi.astype(BF16)
    y = jnp.dot(xb_ref[...], cbd_ref[...], preferred_element_type=F32) + d_ref[...] * u
    y = _gelu(y)
    z = jnp.dot(y.astype(BF16), wg_ref[...], preferred_element_type=F32) + bg_ref[...]
    out = y * _sigmoid(z)
    ms = jnp.mean(out * out, axis=-1, keepdims=True)
    o_ref[...] = (out * lax.rsqrt(ms + RMS_EPS) * ng_ref[...]).astype(BF16)


def _s5_tables(a_re, a_im, b_re, b_im, c_re, c_im, log_step, chunk):
    g, p = a_re.shape
    step = jnp.exp(log_step)[:, None]
    dt_re, dt_im = step * a_re, step * a_im
    mag = jnp.exp(dt_re)
    ab_re, ab_im = mag * jnp.cos(dt_im), mag * jnp.sin(dt_im)
    z_re, z_im = ab_re - 1.0, ab_im
    den = a_re * a_re + a_im * a_im
    f_re = (z_re * a_re + z_im * a_im) / den
    f_im = (z_im * a_re - z_re * a_im) / den
    bb_re = f_re[..., None] * b_re - f_im[..., None] * b_im
    bb_im = f_re[..., None] * b_im + f_im[..., None] * b_re
    eye = jnp.eye(g, dtype=F32)

    def in_map(bb):
        c = bb.shape[-1]
        return jnp.einsum('gpc,gh->gchp', bb, eye).reshape(g * c, g * p)

    def out_map(cc):
        c = cc.shape[1]
        return jnp.einsum('gcp,gh->gphc', cc, eye).reshape(g * p, g * c)

    bbd = jnp.concatenate([in_map(bb_re), in_map(bb_im)], axis=1)
    cbd = jnp.concatenate([out_map(c_re), out_map(-c_im)], axis=0)
    pr = ab_re.reshape(1, g * p)
    pi = ab_im.reshape(1, g * p)
    s = 1
    while s < chunk:
        lr, li = pr[s - 1:s], pi[s - 1:s]
        pr, pi = (jnp.concatenate([pr, pr * lr - pi * li], axis=0),
                  jnp.concatenate([pi, pr * li + pi * lr], axis=0))
        s *= 2
    return bbd.astype(BF16), pr, pi, cbd.astype(BF16)


def _s5(rest, batch, seq, col_block, width, tables, d, w_glu_bf, b_glu, norm_g, chunk):
    bbd, pwr, pwi, cbd = tables
    nstate = pwr.shape[1]
    rest_v = rest.reshape(batch, seq, rest.shape[-1])

    def full(shape):
        return pl.BlockSpec(shape, lambda b, n: (0,) * len(shape))

    out = pl.pallas_call(
        functools.partial(_s5_kernel, nstate=nstate),
        grid=(batch, seq // chunk),
        in_specs=[
            pl.BlockSpec((None, chunk, width), lambda b, n: (b, n, col_block)),
            full((width, 2 * nstate)),
            full((chunk, nstate)),
            full((chunk, nstate)),
            full((2 * nstate, width)),
            full((1, width)),
            full((width, width)),
            full((1, width)),
            full((1, width)),
        ],
        out_specs=pl.BlockSpec((None, chunk, width), lambda b, n: (b, n, 0)),
        out_shape=jax.ShapeDtypeStruct((batch, seq, width), BF16),
        scratch_shapes=[
            pltpu.VMEM((chunk, 2 * nstate), F32),
            pltpu.VMEM((chunk, 2 * nstate), BF16),
            pltpu.VMEM((1, nstate), F32),
            pltpu.VMEM((1, nstate), F32),
        ],
        compiler_params=_params("parallel", "arbitrary"),
        name="s5_scan",
    )(rest_v, bbd, pwr, pwi, cbd, d.reshape(1, width), w_glu_bf, b_glu.reshape(1, width),
      norm_g.reshape(1, width))
    return out.reshape(batch * seq, width)


def _layer_norm_rows(y, g, b):
    mu = jnp.mean(y, axis=-1, keepdims=True)
    yc = y - mu
    var = jnp.mean(yc * yc, axis=-1, keepdims=True)
    return yc * lax.rsqrt(var + LN_EPS) * g + b


def _mix_kernel(*refs, dilations):
    nb = len(dilations)
    o_refs, l_refs = refs[:nb], refs[nb:2 * nb]
    lru_ref, s5_ref, h_ref, w_ref, ga_ref, lg_ref, lb_ref, out_ref, tok_ref = refs[2 * nb:]
    npair = o_refs[0].shape[1]
    tm = h_ref.shape[0]

    def token_major(ref, d, slot, p):
        if d == 1:
            return ref[0, p]
        for c in range(d):
            tok_ref[slot, pl.ds(c, tm // d, stride=d), :] = ref[c, p]
        return tok_ref[slot]

    attn = []
    ssq = jnp.zeros((tm, 1), F32)
    for p in range(npair):
        outs = [token_major(o_refs[i], d, 2 * i, p) for i, d in enumerate(dilations)]
        lses = [token_major(l_refs[i], d, 2 * i + 1, p) for i, d in enumerate(dilations)]
        m = functools.reduce(jnp.maximum, lses)
        es = [jnp.exp(l - m) for l in lses]
        a = sum(e * o for e, o in zip(es, outs)) / sum(es)
        ssq = ssq + jnp.sum(a * a, axis=-1, keepdims=True)
        attn.append(a)
    scale = lax.rsqrt(ssq * (1.0 / (npair * HEAD_PAIR)) + RMS_EPS)
    ga = ga_ref[...]
    attn_n = [(a * scale * ga[:, p * HEAD_PAIR:(p + 1) * HEAD_PAIR]).astype(BF16) for p, a in enumerate(attn)]
    mixed = jnp.concatenate(attn_n + [lru_ref[...], s5_ref[...]], axis=1)
    mix = jnp.dot(mixed, w_ref[...], preferred_element_type=F32)
    y = DEEPNORM_ALPHA * h_ref[...] + mix
    out_ref[...] = _layer_norm_rows(y, lg_ref[...], lb_ref[...])


def _mix(attn_outs, dilations, lru, s5, h, seq, w_out_bf, g_attn, ln_g, ln_b, tm):
    t, d = h.shape
    aw = lru.shape[1]
    sw = s5.shape[1]
    nseq = seq // tm
    npair = aw // HEAD_PAIR

    def rows(w):
        return pl.BlockSpec((tm, w), lambda i: (i, 0))

    def full(shape):
        return pl.BlockSpec(shape, lambda i: (0,) * len(shape))

    def class_major(dil):
        return pl.BlockSpec((None, dil, npair, tm // dil, HEAD_PAIR),
                            lambda i: (i // nseq, 0, 0, i % nseq, 0))

    branch_specs = [class_major(dil) for dil in dilations]
    return pl.pallas_call(
        functools.partial(_mix_kernel, dilations=tuple(dilations)),
        grid=(t // tm,),
        in_specs=branch_specs + branch_specs + [rows(aw), rows(sw), rows(d), full((d, d)),
                                                full((1, aw)), full((1, d)), full((1, d))],
        out_specs=rows(d),
        out_shape=jax.ShapeDtypeStruct((t, d), F32),
        scratch_shapes=[pltpu.VMEM((2 * len(dilations), tm, HEAD_PAIR), F32)],
        compiler_params=_params("parallel"),
        name="mix_out_ln",
    )(*[o for o, _ in attn_outs], *[l for _, l in attn_outs], lru, s5, h, w_out_bf,
      g_attn.reshape(1, aw), ln_g.reshape(1, d), ln_b.reshape(1, d))


def _ffn_kernel(h_ref, hp_ref, wg_ref, wv_ref, cwg_ref, cwv_ref, cbg_ref, cbv_ref, wd_ref,
                lg_ref, lb_ref, out_ref, acc_ref, *, tiles_per_seq):
    i = pl.program_id(0)
    j = pl.program_id(1)
    keep_prev = jnp.where(i % tiles_per_seq == 0, 0.0, 1.0).astype(F32)
    he = jnp.concatenate([hp_ref[...] * keep_prev, h_ref[...]], axis=0).astype(BF16)

    def conv_branch(w_ref, cw_ref, cb_ref):
        up = jnp.dot(he, w_ref[...], preferred_element_type=F32)
        cw = cw_ref[...]
        y = cw[FFN_CONV - 1:FFN_CONV, :] * up
        for k in range(1, FFN_CONV):
            tap = FFN_CONV - 1 - k
            y = y + cw[tap:tap + 1, :] * _shift_rows(up, k)
        return y[SUBLANES:, :] + cb_ref[...]

    gate = conv_branch(wg_ref, cwg_ref, cbg_ref)
    val = conv_branch(wv_ref, cwv_ref, cbv_ref)
    act = (_gelu(gate) * val).astype(BF16)
    part = jnp.dot(act, wd_ref[...], preferred_element_type=F32)

    @pl.when(j == 0)
    def _():
        acc_ref[...] = part

    @pl.when(j > 0)
    def _():
        acc_ref[...] += part

    @pl.when(j == pl.num_programs(1) - 1)
    def _():
        y = DEEPNORM_ALPHA * h_ref[...] + acc_ref[...]
        out_ref[...] = _layer_norm_rows(y, lg_ref[...], lb_ref[...])


def _ffn(h, seq, w_up_bf, conv_w, conv_b, w_down_bf, ln_g, ln_b, tm, cw):
    t, d = h.shape
    d_ff = w_down_bf.shape[0]
    nff = d_ff // cw
    per8 = tm // SUBLANES
    conv_b2 = conv_b.reshape(1, 2 * d_ff)
    return pl.pallas_call(
        functools.partial(_ffn_kernel, tiles_per_seq=seq // tm),
        grid=(t // tm, nff),
        in_specs=[
            pl.BlockSpec((tm, d), lambda i, j: (i, 0)),
            pl.BlockSpec((SUBLANES, d), lambda i, j: (jnp.maximum(i * per8 - 1, 0), 0)),
            pl.BlockSpec((d, cw), lambda i, j: (0, j)),
            pl.BlockSpec((d, cw), lambda i, j: (0, nff + j)),
            pl.BlockSpec((FFN_CONV, cw), lambda i, j: (0, j)),
            pl.BlockSpec((FFN_CONV, cw), lambda i, j: (0, nff + j)),
            pl.BlockSpec((1, cw), lambda i, j: (0, j)),
            pl.BlockSpec((1, cw), lambda i, j: (0, nff + j)),
            pl.BlockSpec((cw, d), lambda i, j: (j, 0)),
            pl.BlockSpec((1, d), lambda i, j: (0, 0)),
            pl.BlockSpec((1, d), lambda i, j: (0, 0)),
        ],
        out_specs=pl.BlockSpec((tm, d), lambda i, j: (i, 0)),
        out_shape=jax.ShapeDtypeStruct((t, d), F32),
        scratch_shapes=[pltpu.VMEM((tm, d), F32)],
        compiler_params=_params("parallel", "arbitrary"),
        name="conv_ffn_ln",
    )(h, h, w_up_bf, w_up_bf, conv_w, conv_w, conv_b2, conv_b2, w_down_bf,
      ln_g.reshape(1, d), ln_b.reshape(1, d))


def kernel(x, w_in, lru_conv_w, lru_conv_b, lru_wr, lru_br, lru_wi, lru_bi, lru_lambda, s5_a_re, s5_a_im, s5_b_re, s5_b_im, s5_c_re, s5_c_im, s5_d, s5_log_step, s5_w_glu, s5_b_glu, mix_norm_g, w_out, ln1_g, ln1_b, w_up, ffn_conv_w, ffn_conv_b, w_down, ln2_g, ln2_b):
    batch, seq, d_model = x.shape
    depth = w_in.shape[0]
    lru_width = lru_conv_w.shape[-1]
    s5_width = s5_w_glu.shape[-1]
    attn_width = (w_in.shape[-1] - 2 * lru_width - s5_width) // 3
    assert attn_width % HEAD_PAIR == 0 and lru_width == attn_width
    assert all(w // d == ATTN_BLOCK for w, d in DILATED_PAIRS)
    dilations = [d for _, d in DILATED_PAIRS]
    s5_chunk = 128
    lru_chunk = 256
    row_tile = 512
    ff_chunk = 512
    s5_col_block = (2 * lru_width) // s5_width
    assert s5_col_block * s5_width == 2 * lru_width

    cos_t, sin_t = _rope_tables(seq)
    h = x.reshape(batch * seq, d_model)
    for l in range(depth):
        qkvs, rest = _proj(h, w_in[l].astype(BF16), cos_t, sin_t, batch, attn_width, dilations, row_tile)
        attn_outs = [_attn_branch(qkv_cm, d) for qkv_cm, d in zip(qkvs, dilations)]
        w_gate = jnp.concatenate([_block_diag(lru_wr[l]), _block_diag(lru_wi[l])], axis=1).astype(BF16)
        b_gate = jnp.concatenate([lru_br[l], lru_bi[l]])
        lru = _lru(rest, batch, seq, lru_width, lru_conv_w[l], lru_conv_b[l], w_gate, b_gate,
                   lru_lambda[l], mix_norm_g[l, attn_width:attn_width + lru_width], lru_chunk)
        tables = _s5_tables(s5_a_re[l], s5_a_im[l], s5_b_re[l], s5_b_im[l], s5_c_re[l], s5_c_im[l],
                            s5_log_step[l], s5_chunk)
        s5 = _s5(rest, batch, seq, s5_col_block, s5_width, tables, s5_d[l].reshape(-1),
                 s5_w_glu[l].astype(BF16), s5_b_glu[l], mix_norm_g[l, attn_width + lru_width:], s5_chunk)
        h1 = _mix(attn_outs, dilations, lru, s5, h, seq, w_out[l].astype(BF16), mix_norm_g[l, :attn_width],
                  ln1_g[l], ln1_b[l], row_tile)
        h = _ffn(h1, seq, w_up[l].astype(BF16), ffn_conv_w[l], ffn_conv_b[l], w_down[l].astype(BF16),
                 ln2_g[l], ln2_b[l], row_tile, ff_chunk)
    return h.reshape(batch, seq, d_model)
```

```python
import functools

import jax
import jax.numpy as jnp
from jax import lax
from jax.experimental import pallas as pl
from jax.experimental.pallas import tpu as pltpu

F32 = jnp.float32
BF16 = jnp.bfloat16

HEAD_DIM = 64
HEAD_PAIR = 2 * HEAD_DIM
DILATED_PAIRS = ((128, 1), (512, 4), (2048, 16))
ATTN_BLOCK = 128
ATTN_QUERIES_PER_STEP = 1024
PROJ_ROW_PARTS = 1
ROPE_THETA = 10000.0
LRU_HEAD = 64
LRU_CONV = 4
LRU_C = 8.0
S5_GROUP = 16
S5_STATE = 64
FFN_CONV = 3
DEPTH = 2
DEEPNORM_ALPHA = (2 * DEPTH) ** 0.25
LN_EPS = 1e-5
RMS_EPS = 1e-6
NEG_BIG = -1e30

LANES = 128
SUBLANES = 8
VMEM_LIMIT = 48 * 1024 * 1024
FFN_VMEM_LIMIT = 56 * 1024 * 1024


def _params(*sem):
    return pltpu.CompilerParams(dimension_semantics=sem, vmem_limit_bytes=VMEM_LIMIT)


def _gelu(x):
    return jax.nn.gelu(x)


def _sigmoid(x):
    return 1.0 / (1.0 + jnp.exp(-x))


def _shift_rows(x, s):
    return pltpu.roll(x, s, 0)


def _run_pitch(run):
    return run + SUBLANES


def _store_runs(scr_ref, slab, x, run):
    pitch = _run_pitch(run)
    for s in range(SUBLANES):
        scr_ref[slab, s * pitch:s * pitch + run, :] = x[s * run:(s + 1) * run, :]


def _load_runs(scr_ref, slab, run):
    pitch = _run_pitch(run)
    return jnp.concatenate([scr_ref[slab, s * pitch:s * pitch + run, :] for s in range(SUBLANES)], axis=0)


def _step_rows(r, run):
    return pl.ds(r, SUBLANES, stride=_run_pitch(run))


def _sublane_delay(v, s, fill):
    sub = lax.broadcasted_iota(jnp.int32, v.shape, 0)
    return jnp.where(sub >= s, pltpu.roll(v, s, 0), fill)


def _proj_kernel(h_ref, w_ref, cos_ref, sin_ref, *refs, attn_width, dilations):
    out_refs = refs[:len(dilations)]
    rest_ref, stage_ref = refs[len(dilations):]
    n_slab = 3 * attn_width // LANES
    n_rot = 2 * attn_width // LANES
    part = h_ref.shape[0] // PROJ_ROW_PARTS
    for part_idx in range(PROJ_ROW_PARTS):
        rows = slice(part_idx * part, (part_idx + 1) * part)
        acc = jnp.dot(h_ref[rows, :].astype(BF16), w_ref[...], preferred_element_type=F32)
        cos = cos_ref[rows, :]
        sin = sin_ref[rows, :]
        lane = lax.broadcasted_iota(jnp.int32, cos.shape, 1)
        first_half = (lane & (HEAD_DIM // 2)) == 0
        for cb in range(n_slab):
            x = acc[:, cb * LANES:(cb + 1) * LANES]
            if cb < n_rot:
                partner = jnp.where(first_half, pltpu.roll(x, LANES - HEAD_DIM // 2, 1),
                                    pltpu.roll(x, HEAD_DIM // 2, 1))
                x = x * cos + partner * sin
            if cb < n_rot // 2:
                x = x * (HEAD_DIM ** -0.5)
            stage_ref[cb, rows, :] = x
        rest_ref[rows, :] = acc[:, 3 * attn_width:]
        for out_ref, d in zip(out_refs, dilations):
            cls_rows = slice(part_idx * (part // d), (part_idx + 1) * (part // d))
            for c in range(d):
                for cb in range(n_slab):
                    if d == 1:
                        piece = stage_ref[cb, rows, :]
                    else:
                        piece = stage_ref[cb, pl.ds(part_idx * part + c, part // d, stride=d), :]
                    out_ref[c, cb, cls_rows, :] = piece.astype(BF16)


def _rope_tables(seq):
    half = HEAD_DIM // 2
    pos = jnp.arange(seq, dtype=F32)
    inv = ROPE_THETA ** (-jnp.arange(half, dtype=F32) * 2.0 / HEAD_DIM)
    ang = pos[:, None] * inv[None, :]
    cos = jnp.cos(ang)
    sin = jnp.sin(ang)
    cos_t = jnp.concatenate([cos, cos, cos, cos], axis=1)
    sin_t = jnp.concatenate([-sin, sin, -sin, sin], axis=1)
    return cos_t, sin_t


def _proj(h, w_in_bf, cos_t, sin_t, batch, attn_width, dilations, tm):
    t, d = h.shape
    d_in = w_in_bf.shape[1]
    seq = cos_t.shape[0]
    nseq = seq // tm
    rest_w = d_in - 3 * attn_width
    n_slab = 3 * attn_width // LANES
    qkv_specs = [pl.BlockSpec((None, dil, n_slab, tm // dil, LANES),
                              lambda i: (i // nseq, 0, 0, i % nseq, 0)) for dil in dilations]
    qkv_shapes = [jax.ShapeDtypeStruct((batch, dil, n_slab, seq // dil, LANES), BF16) for dil in dilations]
    outs = pl.pallas_call(
        functools.partial(_proj_kernel, attn_width=attn_width, dilations=tuple(dilations)),
        grid=(t // tm,),
        in_specs=[
            pl.BlockSpec((tm, d), lambda i: (i, 0)),
            pl.BlockSpec((d, d_in), lambda i: (0, 0)),
            pl.BlockSpec((tm, LANES), lambda i: (i % nseq, 0)),
            pl.BlockSpec((tm, LANES), lambda i: (i % nseq, 0)),
        ],
        out_specs=qkv_specs + [pl.BlockSpec((tm, rest_w), lambda i: (i, 0))],
        out_shape=qkv_shapes + [jax.ShapeDtypeStruct((t, rest_w), F32)],
        scratch_shapes=[pltpu.VMEM((n_slab, tm, LANES), F32)],
        compiler_params=_params("parallel"),
        name="proj_rope",
    )(h, w_in_bf, cos_t, sin_t)
    return outs[:-1], outs[-1]


def _attn_kernel(q_ref, k_ref, v_ref, *rest, has_prev):
    if has_prev:
        kp_ref, vp_ref, o_ref, l_ref = rest
    else:
        o_ref, l_ref = rest
    ncls, qb, _ = q_ref.shape
    win = (ATTN_BLOCK, 2 * ATTN_BLOCK)
    row = lax.broadcasted_iota(jnp.int32, win, 0)
    col = lax.broadcasted_iota(jnp.int32, win, 1)
    band_bias = jnp.where((col >= row) & (col <= row + ATTN_BLOCK), 0.0, NEG_BIG).astype(F32)
    if has_prev:
        no_prev = jnp.where(pl.program_id(3) > 0, 0.0, NEG_BIG).astype(F32)
        first_bias = band_bias + jnp.where(col < ATTN_BLOCK, no_prev, 0.0)
    else:
        first_bias = jnp.where(col < ATTN_BLOCK, NEG_BIG, band_bias).astype(F32)
    head0_q = lax.broadcasted_iota(jnp.int32, (ATTN_BLOCK, HEAD_PAIR), 1) < HEAD_DIM
    contract_last = (((1,), (1,)), ((), ()))
    for cls in range(ncls):
        for j in range(qb // ATTN_BLOCK):
            rows = slice(j * ATTN_BLOCK, (j + 1) * ATTN_BLOCK)
            q = q_ref[cls, rows, :]
            if j > 0:
                wrows = slice((j - 1) * ATTN_BLOCK, (j + 1) * ATTN_BLOCK)
                kw = k_ref[cls, wrows, :]
                vw = v_ref[cls, wrows, :]
                bias = band_bias
            elif has_prev:
                kw = jnp.concatenate([kp_ref[cls], k_ref[cls, rows, :]], axis=0)
                vw = jnp.concatenate([vp_ref[cls], v_ref[cls, rows, :]], axis=0)
                bias = first_bias
            else:
                kw = jnp.concatenate([k_ref[cls, rows, :]] * 2, axis=0)
                vw = jnp.concatenate([v_ref[cls, rows, :]] * 2, axis=0)
                bias = first_bias
            head0_kv = lax.broadcasted_iota(jnp.int32, vw.shape, 1) < HEAD_DIM
            pvs = []
            maxes = []
            for hd in range(2):
                sel_q = head0_q if hd == 0 else jnp.logical_not(head0_q)
                sel_kv = head0_kv if hd == 0 else jnp.logical_not(head0_kv)
                qh = jnp.where(sel_q, q, jnp.zeros_like(q))
                s = lax.dot_general(qh, kw, contract_last, preferred_element_type=F32) + bias
                m = jnp.max(s, axis=1, keepdims=True)
                p = jnp.exp(s - m).astype(BF16)
                v_aug = jnp.where(sel_kv, vw, jnp.ones_like(vw))
                pvs.append(jnp.dot(p, v_aug, preferred_element_type=F32))
                maxes.append(m)
            num = jnp.where(head0_q, pvs[0], pvs[1])
            den = pltpu.roll(jnp.where(head0_q, pvs[1], pvs[0]), HEAD_DIM, 1)
            o_ref[cls, rows, :] = num / den
            l_ref[cls, rows, :] = jnp.where(head0_q, maxes[0], maxes[1]) + jnp.log(den)


def _attn_branch(qkv_cm, dilation):
    batch, _, n_slab, m, _ = qkv_cm.shape
    npair = n_slab // 3
    qb = min(ATTN_QUERIES_PER_STEP, m)
    ncls = min(ATTN_QUERIES_PER_STEP // qb, dilation)
    nsub = qb // ATTN_BLOCK
    nblk = m // qb
    has_prev = nblk > 1

    def own(off):
        return pl.BlockSpec((None, ncls, None, qb, HEAD_PAIR),
                            lambda b, c, p, n: (b, c, off * npair + p, n, 0))

    def prev(off):
        return pl.BlockSpec((None, ncls, None, ATTN_BLOCK, HEAD_PAIR),
                            lambda b, c, p, n: (b, c, off * npair + p, jnp.maximum(n * nsub - 1, 0), 0))

    out_spec = pl.BlockSpec((None, ncls, None, qb, HEAD_PAIR), lambda b, c, p, n: (b, c, p, n, 0))
    out_sds = jax.ShapeDtypeStruct((batch, dilation, npair, m, HEAD_PAIR), F32)
    in_specs = [own(0), own(1), own(2)] + ([prev(1), prev(2)] if has_prev else [])
    return pl.pallas_call(
        functools.partial(_attn_kernel, has_prev=has_prev),
        grid=(batch, dilation // ncls, npair, nblk),
        in_specs=in_specs,
        out_specs=[out_spec, out_spec],
        out_shape=[out_sds, out_sds],
        compiler_params=_params("parallel", "parallel", "parallel", "arbitrary"),
        name=f"dilated_attn_d{dilation}",
    )(*([qkv_cm] * len(in_specs)))


def _lru_kernel(x_ref, xp_ref, g_ref, cw_ref, cb_ref, w_ref, b_ref, lam_ref, ng_ref,
                o_ref, carry_ref, a_scr, u_scr, *, width):
    n = pl.program_id(1)

    @pl.when(n == 0)
    def _():
        carry_ref[...] = jnp.zeros_like(carry_ref)

    x = x_ref[...]
    rows = x.shape[0]
    xp = xp_ref[...] * jnp.where(n > 0, 1.0, 0.0).astype(F32)
    xe = jnp.concatenate([xp, x], axis=0)
    cw = cw_ref[...]
    xc = cw[LRU_CONV - 1:LRU_CONV, :] * x + cb_ref[...]
    for k in range(1, LRU_CONV):
        tap = LRU_CONV - 1 - k
        xc = xc + cw[tap:tap + 1, :] * _shift_rows(xe, k)[SUBLANES:, :]
    gates = jnp.dot(xc.astype(BF16), w_ref[...], preferred_element_type=F32) + b_ref[...]
    r = _sigmoid(gates[:, :width])
    i = _sigmoid(gates[:, width:])
    lam = lam_ref[...]
    softplus_neg = jnp.maximum(-lam, 0.0) + jnp.log1p(jnp.exp(-jnp.abs(lam)))
    log_a = (-LRU_C) * r * softplus_neg
    a = jnp.exp(log_a)
    t = jnp.tanh(log_a)
    u = jnp.sqrt(-2.0 * t / (1.0 - t)) * (i * xc)
    run = rows // SUBLANES
    pitch = _run_pitch(run)
    h_slabs = []
    for sl in range(width // LANES):
        cols = slice(sl * LANES, (sl + 1) * LANES)
        _store_runs(a_scr, sl, a[:, cols], run)
        _store_runs(u_scr, sl, u[:, cols], run)
        h_r = a_cum = None
        for r in range(run):
            step = _step_rows(r, run)
            a_r = a_scr[sl, step, :]
            u_r = u_scr[sl, step, :]
            if r == 0:
                h_r, a_cum = u_r, a_r
            else:
                h_r, a_cum = a_r * h_r + u_r, a_r * a_cum
                u_scr[sl, step, :] = h_r
                a_scr[sl, step, :] = a_cum
        p_inc, e_inc = a_cum, h_r
        for sh in (1, 2, 4):
            e_inc, p_inc = (p_inc * _sublane_delay(e_inc, sh, 0.0) + e_inc,
                            p_inc * _sublane_delay(p_inc, sh, 1.0))
        carry = carry_ref[:, cols]
        state_in = _sublane_delay(e_inc, 1, 0.0) + _sublane_delay(p_inc, 1, 1.0) * carry
        carry_ref[:, cols] = (e_inc + p_inc * carry)[SUBLANES - 1:SUBLANES, :]
        pieces = []
        for s in range(SUBLANES):
            blk = slice(s * pitch, s * pitch + run)
            pieces.append(u_scr[sl, blk, :] + a_scr[sl, blk, :] * state_in[s:s + 1, :])
        h_slabs.append(jnp.concatenate(pieces, axis=0))
    h = jnp.concatenate(h_slabs, axis=1)
    out = h * _gelu(g_ref[...])
    ms = jnp.mean(out * out, axis=-1, keepdims=True)
    o_ref[...] = (out * lax.rsqrt(ms + RMS_EPS) * ng_ref[...]).astype(BF16)


def _lru(rest, batch, seq, width, conv_w, conv_b, w_gate_bf, b_gate, lam, norm_g, chunk):
    rest_v = rest.reshape(batch, seq, rest.shape[-1])
    nchunk = seq // chunk
    per8 = chunk // SUBLANES

    def full(shape):
        return pl.BlockSpec(shape, lambda b, n: (0,) * len(shape))

    out = pl.pallas_call(
        functools.partial(_lru_kernel, width=width),
        grid=(batch, nchunk),
        in_specs=[
            pl.BlockSpec((None, chunk, width), lambda b, n: (b, n, 0)),
            pl.BlockSpec((None, SUBLANES, width), lambda b, n: (b, jnp.maximum(n * per8 - 1, 0), 0)),
            pl.BlockSpec((None, chunk, width), lambda b, n: (b, n, 1)),
            full((LRU_CONV, width)),
            full((1, width)),
            full((width, 2 * width)),
            full((1, 2 * width)),
            full((1, width)),
            full((1, width)),
        ],
        out_specs=pl.BlockSpec((None, chunk, width), lambda b, n: (b, n, 0)),
        out_shape=jax.ShapeDtypeStruct((batch, seq, width), BF16),
        scratch_shapes=[pltpu.VMEM((1, width), F32)]
        + [pltpu.VMEM((width // LANES, SUBLANES * _run_pitch(chunk // SUBLANES), LANES), F32)] * 2,
        compiler_params=_params("parallel", "arbitrary"),
        name="rg_lru",
    )(rest_v, rest_v, rest_v, conv_w, conv_b.reshape(1, width), w_gate_bf,
      b_gate.reshape(1, 2 * width), lam.reshape(1, width), norm_g.reshape(1, width))
    return out.reshape(batch * seq, width)


def _block_diag(w):
    h, i, j = w.shape
    eye = jnp.eye(h, dtype=w.dtype)
    return jnp.einsum('hij,hg->higj', w, eye).reshape(h * i, h * j)


def _cmul(ar, ai, br, bi):
    return ar * br - ai * bi, ar * bi + ai * br


def _s5_kernel(u_ref, bbd_ref, pwr_ref, pwi_ref, mur_ref, mui_ref, cbd_ref, d_ref, wg_ref, bg_ref, ng_ref,
               o_ref, perm_ref, bu_ref, xb_ref, cr_ref, ci_ref, *, nstate):
    n = pl.program_id(1)

    @pl.when(n == 0)
    def _():
        cr_ref[...] = jnp.zeros_like(cr_ref)
        ci_ref[...] = jnp.zeros_like(ci_ref)

    u = u_ref[...]
    rows, width = u.shape
    run = rows // SUBLANES
    slabs = []
    for sl in range(width // LANES):
        _store_runs(perm_ref, sl, u[:, sl * LANES:(sl + 1) * LANES], run)
        slabs.append(jnp.concatenate([perm_ref[sl, _step_rows(r, run), :] for r in range(run)], axis=0))
    u_run = jnp.concatenate(slabs, axis=1)
    bu_ref[...] = jnp.dot(u_run.astype(BF16), bbd_ref[...], preferred_element_type=F32)
    for j in range(nstate // LANES):
        re_cols = slice(j * LANES, (j + 1) * LANES)
        im_cols = slice(nstate + j * LANES, nstate + (j + 1) * LANES)
        lr = pwr_ref[0:1, re_cols]
        li = pwi_ref[0:1, re_cols]
        xr = bu_ref[0:SUBLANES, re_cols]
        xi = bu_ref[0:SUBLANES, im_cols]
        for r in range(1, run):
            step = slice(r * SUBLANES, (r + 1) * SUBLANES)
            dr, di = _cmul(lr, li, xr, xi)
            xr = bu_ref[step, re_cols] + dr
            xi = bu_ref[step, im_cols] + di
            bu_ref[step, re_cols] = xr
            bu_ref[step, im_cols] = xi
        er, ei = xr, xi
        for sh in (1, 2, 4):
            k = run * sh - 1
            dr, di = _cmul(pwr_ref[k:k + 1, re_cols], pwi_ref[k:k + 1, re_cols],
                           _sublane_delay(er, sh, 0.0), _sublane_delay(ei, sh, 0.0))
            er, ei = er + dr, ei + di
        cr = cr_ref[:, re_cols]
        ci = ci_ref[:, re_cols]
        dr, di = _cmul(mur_ref[:, re_cols], mui_ref[:, re_cols], cr, ci)
        in_r = _sublane_delay(er, 1, 0.0) + dr
        in_i = _sublane_delay(ei, 1, 0.0) + di
        dr, di = _cmul(pwr_ref[rows - 1:rows, re_cols], pwi_ref[rows - 1:rows, re_cols], cr, ci)
        cr_ref[:, re_cols] = (er + dr)[SUBLANES - 1:SUBLANES, :]
        ci_ref[:, re_cols] = (ei + di)[SUBLANES - 1:SUBLANES, :]
        for r0 in range(0, run, 2):
            outs_r, outs_i = [], []
            for r in (r0, r0 + 1):
                step = slice(r * SUBLANES, (r + 1) * SUBLANES)
                dr, di = _cmul(pwr_ref[r:r + 1, re_cols], pwi_ref[r:r + 1, re_cols], in_r, in_i)
                outs_r.append(bu_ref[step, re_cols] + dr)
                outs_i.append(bu_ref[step, im_cols] + di)
            pair = slice(r0 * SUBLANES, (r0 + 2) * SUBLANES)
            xb_ref[pair, re_cols] = jnp.concatenate(outs_r, axis=0).astype(BF16)
            xb_ref[pair, im_cols] = jnp.concatenate(outs_i, axis=0).astype(BF16)
    y = jnp.dot(xb_ref[...], cbd_ref[...], preferred_element_type=F32) + d_ref[...] * u_run
    y = _gelu(y)
    z = jnp.dot(y.astype(BF16), wg_ref[...], preferred_element_type=F32) + bg_ref[...]
    out = y * _sigmoid(z)
    ms = jnp.mean(out * out, axis=-1, keepdims=True)
    out = out * lax.rsqrt(ms + RMS_EPS) * ng_ref[...]
    slabs = []
    for sl in range(width // LANES):
        for r in range(run):
            perm_ref[sl, _step_rows(r, run), :] = out[r * SUBLANES:(r + 1) * SUBLANES, sl * LANES:(sl + 1) * LANES]
        slabs.append(_load_runs(perm_ref, sl, run))
    o_ref[...] = jnp.concatenate(slabs, axis=1).astype(BF16)


def _s5_tables(a_re, a_im, b_re, b_im, c_re, c_im, log_step, chunk):
    g, p = a_re.shape
    step = jnp.exp(log_step)[:, None]
    dt_re, dt_im = step * a_re, step * a_im
    mag = jnp.exp(dt_re)
    ab_re, ab_im = mag * jnp.cos(dt_im), mag * jnp.sin(dt_im)
    z_re, z_im = ab_re - 1.0, ab_im
    den = a_re * a_re + a_im * a_im
    f_re = (z_re * a_re + z_im * a_im) / den
    f_im = (z_im * a_re - z_re * a_im) / den
    bb_re = f_re[..., None] * b_re - f_im[..., None] * b_im
    bb_im = f_re[..., None] * b_im + f_im[..., None] * b_re
    eye = jnp.eye(g, dtype=F32)

    def in_map(bb):
        c = bb.shape[-1]
        return jnp.einsum('gpc,gh->gchp', bb, eye).reshape(g * c, g * p)

    def out_map(cc):
        c = cc.shape[1]
        return jnp.einsum('gcp,gh->gphc', cc, eye).reshape(g * p, g * c)

    bbd = jnp.concatenate([in_map(bb_re), in_map(bb_im)], axis=1)
    cbd = jnp.concatenate([out_map(c_re), out_map(-c_im)], axis=0)
    pr = ab_re.reshape(1, g * p)
    pi = ab_im.reshape(1, g * p)
    s = 1
    while s < chunk:
        lr, li = pr[s - 1:s], pi[s - 1:s]
        pr, pi = (jnp.concatenate([pr, pr * lr - pi * li], axis=0),
                  jnp.concatenate([pi, pr * li + pi * lr], axis=0))
        s *= 2
    run = chunk // SUBLANES
    run_rows = jnp.arange(1, SUBLANES) * run - 1
    mur = jnp.concatenate([jnp.ones((1, g * p), F32), pr[run_rows]], axis=0)
    mui = jnp.concatenate([jnp.zeros((1, g * p), F32), pi[run_rows]], axis=0)
    return bbd.astype(BF16), pr, pi, mur, mui, cbd.astype(BF16)


def _s5(rest, batch, seq, col_block, width, tables, d, w_glu_bf, b_glu, norm_g, chunk):
    bbd, pwr, pwi, mur, mui, cbd = tables
    nstate = pwr.shape[1]
    rest_v = rest.reshape(batch, seq, rest.shape[-1])

    def full(shape):
        return pl.BlockSpec(shape, lambda b, n: (0,) * len(shape))

    out = pl.pallas_call(
        functools.partial(_s5_kernel, nstate=nstate),
        grid=(batch, seq // chunk),
        in_specs=[
            pl.BlockSpec((None, chunk, width), lambda b, n: (b, n, col_block)),
            full((width, 2 * nstate)),
            full((chunk, nstate)),
            full((chunk, nstate)),
            full((SUBLANES, nstate)),
            full((SUBLANES, nstate)),
            full((2 * nstate, width)),
            full((1, width)),
            full((width, width)),
            full((1, width)),
            full((1, width)),
        ],
        out_specs=pl.BlockSpec((None, chunk, width), lambda b, n: (b, n, 0)),
        out_shape=jax.ShapeDtypeStruct((batch, seq, width), BF16),
        scratch_shapes=[
            pltpu.VMEM((width // LANES, SUBLANES * _run_pitch(chunk // SUBLANES), LANES), F32),
            pltpu.VMEM((chunk, 2 * nstate), F32),
            pltpu.VMEM((chunk, 2 * nstate), BF16),
            pltpu.VMEM((1, nstate), F32),
            pltpu.VMEM((1, nstate), F32),
        ],
        compiler_params=_params("parallel", "arbitrary"),
        name="s5_scan",
    )(rest_v, bbd, pwr, pwi, mur, mui, cbd, d.reshape(1, width), w_glu_bf, b_glu.reshape(1, width),
      norm_g.reshape(1, width))
    return out.reshape(batch * seq, width)


def _layer_norm_rows(y, g, b):
    mu = jnp.mean(y, axis=-1, keepdims=True)
    yc = y - mu
    var = jnp.mean(yc * yc, axis=-1, keepdims=True)
    return yc * lax.rsqrt(var + LN_EPS) * g + b


def _mix_kernel(*refs, dilations):
    nb = len(dilations)
    o_refs, l_refs = refs[:nb], refs[nb:2 * nb]
    lru_ref, s5_ref, h_ref, w_ref, ga_ref, lg_ref, lb_ref, out_ref, tok_ref = refs[2 * nb:]
    npair = o_refs[0].shape[1]
    tm = h_ref.shape[0]

    def token_major(ref, d, slot, p):
        if d == 1:
            return ref[0, p]
        for c in range(d):
            tok_ref[slot, pl.ds(c, tm // d, stride=d), :] = ref[c, p]
        return tok_ref[slot]

    attn = []
    ssq = jnp.zeros((tm, 1), F32)
    for p in range(npair):
        outs = [token_major(o_refs[i], d, 2 * i, p) for i, d in enumerate(dilations)]
        lses = [token_major(l_refs[i], d, 2 * i + 1, p) for i, d in enumerate(dilations)]
        m = functools.reduce(jnp.maximum, lses)
        es = [jnp.exp(l - m) for l in lses]
        a = sum(e * o for e, o in zip(es, outs)) / sum(es)
        ssq = ssq + jnp.sum(a * a, axis=-1, keepdims=True)
        attn.append(a)
    scale = lax.rsqrt(ssq * (1.0 / (npair * HEAD_PAIR)) + RMS_EPS)
    ga = ga_ref[...]
    attn_n = [(a * scale * ga[:, p * HEAD_PAIR:(p + 1) * HEAD_PAIR]).astype(BF16) for p, a in enumerate(attn)]
    mixed = jnp.concatenate(attn_n + [lru_ref[...], s5_ref[...]], axis=1)
    mix = jnp.dot(mixed, w_ref[...], preferred_element_type=F32)
    y = DEEPNORM_ALPHA * h_ref[...] + mix
    out_ref[...] = _layer_norm_rows(y, lg_ref[...], lb_ref[...])


def _mix(attn_outs, dilations, lru, s5, h, seq, w_out_bf, g_attn, ln_g, ln_b, tm):
    t, d = h.shape
    aw = lru.shape[1]
    sw = s5.shape[1]
    nseq = seq // tm
    npair = aw // HEAD_PAIR

    def rows(w):
        return pl.BlockSpec((tm, w), lambda i: (i, 0))

    def full(shape):
        return pl.BlockSpec(shape, lambda i: (0,) * len(shape))

    def class_major(dil):
        return pl.BlockSpec((None, dil, npair, tm // dil, HEAD_PAIR),
                            lambda i: (i // nseq, 0, 0, i % nseq, 0))

    branch_specs = [class_major(dil) for dil in dilations]
    return pl.pallas_call(
        functools.partial(_mix_kernel, dilations=tuple(dilations)),
        grid=(t // tm,),
        in_specs=branch_specs + branch_specs + [rows(aw), rows(sw), rows(d), full((d, d)),
                                                full((1, aw)), full((1, d)), full((1, d))],
        out_specs=rows(d),
        out_shape=jax.ShapeDtypeStruct((t, d), F32),
        scratch_shapes=[pltpu.VMEM((2 * len(dilations), tm, HEAD_PAIR), F32)],
        compiler_params=_params("parallel"),
        name="mix_out_ln",
    )(*[o for o, _ in attn_outs], *[l for _, l in attn_outs], lru, s5, h, w_out_bf,
      g_attn.reshape(1, aw), ln_g.reshape(1, d), ln_b.reshape(1, d))


def _ffn_kernel(h_ref, hp_ref, wu_ref, cw_ref, cb_ref, wd_ref, lg_ref, lb_ref, out_ref,
                *, tiles_per_seq, ff_chunk):
    i = pl.program_id(0)
    keep_prev = jnp.where(i % tiles_per_seq == 0, 0.0, 1.0).astype(F32)
    h = h_ref[...]
    he = jnp.concatenate([hp_ref[...] * keep_prev, h], axis=0).astype(BF16)
    d_ff = wd_ref.shape[0]

    def conv_branch(col0):
        cols = slice(col0, col0 + ff_chunk)
        up = jnp.dot(he, wu_ref[:, cols], preferred_element_type=F32)
        cw = cw_ref[:, cols]
        y = cw[FFN_CONV - 1:FFN_CONV, :] * up
        for k in range(1, FFN_CONV):
            tap = FFN_CONV - 1 - k
            y = y + cw[tap:tap + 1, :] * _shift_rows(up, k)
        return y[SUBLANES:, :] + cb_ref[:, cols]

    acc = None
    for j in range(d_ff // ff_chunk):
        gate = conv_branch(j * ff_chunk)
        val = conv_branch(d_ff + j * ff_chunk)
        act = (_gelu(gate) * val).astype(BF16)
        part = jnp.dot(act, wd_ref[j * ff_chunk:(j + 1) * ff_chunk, :], preferred_element_type=F32)
        acc = part if acc is None else acc + part
    y = DEEPNORM_ALPHA * h + acc
    out_ref[...] = _layer_norm_rows(y, lg_ref[...], lb_ref[...])


def _ffn(h, seq, w_up_bf, conv_w, conv_b, w_down_bf, ln_g, ln_b, tm, cw):
    t, d = h.shape
    d_ff = w_down_bf.shape[0]
    per8 = tm // SUBLANES

    def resident(shape):
        return pl.BlockSpec(shape, lambda i: (0,) * len(shape), pipeline_mode=pl.Buffered(1))

    return pl.pallas_call(
        functools.partial(_ffn_kernel, tiles_per_seq=seq // tm, ff_chunk=cw),
        grid=(t // tm,),
        in_specs=[
            pl.BlockSpec((tm, d), lambda i: (i, 0)),
            pl.BlockSpec((SUBLANES, d), lambda i: (jnp.maximum(i * per8 - 1, 0), 0)),
            resident((d, 2 * d_ff)),
            resident((FFN_CONV, 2 * d_ff)),
            resident((1, 2 * d_ff)),
            resident((d_ff, d)),
            resident((1, d)),
            resident((1, d)),
        ],
        out_specs=pl.BlockSpec((tm, d), lambda i: (i, 0)),
        out_shape=jax.ShapeDtypeStruct((t, d), F32),
        compiler_params=pltpu.CompilerParams(dimension_semantics=("parallel",),
                                             vmem_limit_bytes=FFN_VMEM_LIMIT),
        name="conv_ffn_ln",
    )(h, h, w_up_bf, conv_w, conv_b.reshape(1, 2 * d_ff), w_down_bf,
      ln_g.reshape(1, d), ln_b.reshape(1, d))


def kernel(x, w_in, lru_conv_w, lru_conv_b, lru_wr, lru_br, lru_wi, lru_bi, lru_lambda, s5_a_re, s5_a_im, s5_b_re, s5_b_im, s5_c_re, s5_c_im, s5_d, s5_log_step, s5_w_glu, s5_b_glu, mix_norm_g, w_out, ln1_g, ln1_b, w_up, ffn_conv_w, ffn_conv_b, w_down, ln2_g, ln2_b):
    batch, seq, d_model = x.shape
    depth = w_in.shape[0]
    lru_width = lru_conv_w.shape[-1]
    s5_width = s5_w_glu.shape[-1]
    attn_width = (w_in.shape[-1] - 2 * lru_width - s5_width) // 3
    assert attn_width % HEAD_PAIR == 0 and lru_width == attn_width
    assert all(w // d == ATTN_BLOCK for w, d in DILATED_PAIRS)
    dilations = [d for _, d in DILATED_PAIRS]
    s5_chunk = 256
    lru_chunk = 256
    row_tile = 512
    ff_chunk = 512
    s5_col_block = (2 * lru_width) // s5_width
    assert s5_col_block * s5_width == 2 * lru_width

    cos_t, sin_t = _rope_tables(seq)
    h = x.reshape(batch * seq, d_model)
    for l in range(depth):
        qkvs, rest = _proj(h, w_in[l].astype(BF16), cos_t, sin_t, batch, attn_width, dilations, row_tile)
        attn_outs = [_attn_branch(qkv_cm, d) for qkv_cm, d in zip(qkvs, dilations)]
        w_gate = jnp.concatenate([_block_diag(lru_wr[l]), _block_diag(lru_wi[l])], axis=1).astype(BF16)
        b_gate = jnp.concatenate([lru_br[l], lru_bi[l]])
        lru = _lru(rest, batch, seq, lru_width, lru_conv_w[l], lru_conv_b[l], w_gate, b_gate,
                   lru_lambda[l], mix_norm_g[l, attn_width:attn_width + lru_width], lru_chunk)
        tables = _s5_tables(s5_a_re[l], s5_a_im[l], s5_b_re[l], s5_b_im[l], s5_c_re[l], s5_c_im[l],
                            s5_log_step[l], s5_chunk)
        s5 = _s5(rest, batch, seq, s5_col_block, s5_width, tables, s5_d[l].reshape(-1),
                 s5_w_glu[l].astype(BF16), s5_b_glu[l], mix_norm_g[l, attn_width + lru_width:], s5_chunk)
        h1 = _mix(attn_outs, dilations, lru, s5, h, seq, w_out[l].astype(BF16), mix_norm_g[l, :attn_width],
                  ln1_g[l], ln1_b[l], row_tile)
        h = _ffn(h1, seq, w_up[l].astype(BF16), ffn_conv_w[l], ffn_conv_b[l], w_down[l].astype(BF16),
                 ln2_g[l], ln2_b[l], row_tile, ff_chunk)
    return h.reshape(batch, seq, d_model)
```

```python
import functools

import jax
import jax.numpy as jnp
from jax import lax
from jax.experimental import pallas as pl
from jax.experimental.pallas import tpu as pltpu

F32 = jnp.float32
BF16 = jnp.bfloat16

HEAD_DIM = 64
HEAD_PAIR = 2 * HEAD_DIM
DILATED_PAIRS = ((128, 1), (512, 4), (2048, 16))
ATTN_BLOCK = 128
ATTN_QUERIES_PER_STEP = 1024
SEQS_PER_STEP = 2
ATTN_PV_LAG = 2
ROPE_THETA = 10000.0
LRU_HEAD = 64
LRU_CONV = 4
LRU_C = 8.0
S5_GROUP = 16
S5_STATE = 64
FFN_CONV = 3
DEPTH = 2
DEEPNORM_ALPHA = (2 * DEPTH) ** 0.25
LN_EPS = 1e-5
RMS_EPS = 1e-6
NEG_BIG = -1e30

LANES = 128
SUBLANES = 8
VMEM_LIMIT = 48 * 1024 * 1024
FFN_VMEM_LIMIT = 56 * 1024 * 1024


def _params(*sem):
    return pltpu.CompilerParams(dimension_semantics=sem, vmem_limit_bytes=VMEM_LIMIT)


def _gelu(x):
    return jax.nn.gelu(x)


def _sigmoid(x):
    return 1.0 / (1.0 + jnp.exp(-x))


def _shift_rows(x, s):
    return pltpu.roll(x, s, 0)


def _run_pitch(run):
    return run + SUBLANES


def _store_runs(scr_ref, slab, x, run):
    pitch = _run_pitch(run)
    for s in range(SUBLANES):
        scr_ref[slab, s * pitch:s * pitch + run, :] = x[s * run:(s + 1) * run, :]


def _load_runs(scr_ref, slab, run):
    pitch = _run_pitch(run)
    return jnp.concatenate([scr_ref[slab, s * pitch:s * pitch + run, :] for s in range(SUBLANES)], axis=0)


def _step_rows(r, run):
    return pl.ds(r, SUBLANES, stride=_run_pitch(run))


def _sublane_delay(v, s, fill):
    sub = lax.broadcasted_iota(jnp.int32, v.shape, 0)
    return jnp.where(sub >= s, pltpu.roll(v, s, 0), fill)


def _proj_kernel(h_ref, w_ref, cos_ref, sin_ref, *refs, attn_width, dilations):
    out_refs = refs[:len(dilations)]
    rest_ref, stage_ref = refs[len(dilations):]
    n_slab = 3 * attn_width // LANES
    n_rot = 2 * attn_width // LANES
    rows = h_ref.shape[0]
    acc = jnp.dot(h_ref[...].astype(BF16), w_ref[...], preferred_element_type=F32)
    cos = cos_ref[...]
    sin = sin_ref[...]
    lane = lax.broadcasted_iota(jnp.int32, cos.shape, 1)
    first_half = (lane & (HEAD_DIM // 2)) == 0
    for cb in range(n_slab):
        x = acc[:, cb * LANES:(cb + 1) * LANES]
        if cb < n_rot:
            partner = jnp.where(first_half, pltpu.roll(x, LANES - HEAD_DIM // 2, 1),
                                pltpu.roll(x, HEAD_DIM // 2, 1))
            x = x * cos + partner * sin
        if cb < n_rot // 2:
            x = x * (HEAD_DIM ** -0.5)
        stage_ref[cb] = x
    rest_ref[...] = acc[:, 3 * attn_width:]
    for out_ref, d in zip(out_refs, dilations):
        for c in range(d):
            for cb in range(n_slab):
                if d == 1:
                    piece = stage_ref[cb]
                else:
                    piece = stage_ref[cb, pl.ds(c, rows // d, stride=d), :]
                out_ref[c, cb] = piece.astype(BF16)


def _rope_tables(seq):
    half = HEAD_DIM // 2
    pos = jnp.arange(seq, dtype=F32)
    inv = ROPE_THETA ** (-jnp.arange(half, dtype=F32) * 2.0 / HEAD_DIM)
    ang = pos[:, None] * inv[None, :]
    cos = jnp.cos(ang)
    sin = jnp.sin(ang)
    cos_t = jnp.concatenate([cos, cos, cos, cos], axis=1)
    sin_t = jnp.concatenate([-sin, sin, -sin, sin], axis=1)
    return cos_t, sin_t


def _proj(h, w_in_bf, cos_t, sin_t, batch, attn_width, dilations, tm):
    t, d = h.shape
    d_in = w_in_bf.shape[1]
    seq = cos_t.shape[0]
    nseq = seq // tm
    rest_w = d_in - 3 * attn_width
    n_slab = 3 * attn_width // LANES
    qkv_specs = [pl.BlockSpec((None, dil, n_slab, tm // dil, LANES),
                              lambda i: (i // nseq, 0, 0, i % nseq, 0)) for dil in dilations]
    qkv_shapes = [jax.ShapeDtypeStruct((batch, dil, n_slab, seq // dil, LANES), BF16) for dil in dilations]
    outs = pl.pallas_call(
        functools.partial(_proj_kernel, attn_width=attn_width, dilations=tuple(dilations)),
        grid=(t // tm,),
        in_specs=[
            pl.BlockSpec((tm, d), lambda i: (i, 0)),
            pl.BlockSpec((d, d_in), lambda i: (0, 0)),
            pl.BlockSpec((tm, LANES), lambda i: (i % nseq, 0)),
            pl.BlockSpec((tm, LANES), lambda i: (i % nseq, 0)),
        ],
        out_specs=qkv_specs + [pl.BlockSpec((tm, rest_w), lambda i: (i, 0))],
        out_shape=qkv_shapes + [jax.ShapeDtypeStruct((t, rest_w), F32)],
        scratch_shapes=[pltpu.VMEM((n_slab, tm, LANES), F32)],
        compiler_params=_params("parallel"),
        name="proj_rope",
    )(h, w_in_bf, cos_t, sin_t)
    return outs[:-1], outs[-1]


def _attn_kernel(q_ref, k_ref, v_ref, *rest, has_prev):
    if has_prev:
        kp_ref, vp_ref, o_ref, l_ref = rest
    else:
        o_ref, l_ref = rest
    ncls, qb, _ = q_ref.shape
    win = (ATTN_BLOCK, 2 * ATTN_BLOCK)
    row = lax.broadcasted_iota(jnp.int32, win, 0)
    col = lax.broadcasted_iota(jnp.int32, win, 1)
    band_bias = jnp.where((col >= row) & (col <= row + ATTN_BLOCK), 0.0, NEG_BIG).astype(F32)
    if has_prev:
        no_prev = jnp.where(pl.program_id(3) > 0, 0.0, NEG_BIG).astype(F32)
        first_bias = band_bias + jnp.where(col < ATTN_BLOCK, no_prev, 0.0)
    else:
        first_bias = jnp.where(col < ATTN_BLOCK, NEG_BIG, band_bias).astype(F32)
    head0_q = lax.broadcasted_iota(jnp.int32, (ATTN_BLOCK, HEAD_PAIR), 1) < HEAD_DIM
    head0_kv = lax.broadcasted_iota(jnp.int32, (2 * ATTN_BLOCK, HEAD_PAIR), 1) < HEAD_DIM
    contract_last = (((1,), (1,)), ((), ()))
    items = [(cls, j, hd) for cls in range(ncls) for j in range(qb // ATTN_BLOCK) for hd in range(2)]

    def window(ref, prev_ref, cls, j):
        rows = slice(j * ATTN_BLOCK, (j + 1) * ATTN_BLOCK)
        if j > 0:
            return ref[cls, (j - 1) * ATTN_BLOCK:(j + 1) * ATTN_BLOCK, :]
        if has_prev:
            return jnp.concatenate([prev_ref[cls], ref[cls, rows, :]], axis=0)
        return jnp.concatenate([ref[cls, rows, :]] * 2, axis=0)

    def scores(cls, j, hd):
        q = q_ref[cls, j * ATTN_BLOCK:(j + 1) * ATTN_BLOCK, :]
        sel_q = head0_q if hd == 0 else jnp.logical_not(head0_q)
        qh = jnp.where(sel_q, q, jnp.zeros_like(q))
        kw = window(k_ref, kp_ref if has_prev else None, cls, j)
        bias = band_bias if j > 0 else first_bias
        return lax.dot_general(qh, kw, contract_last, preferred_element_type=F32) + bias

    def softmax_numerators(s):
        m = jnp.max(s, axis=1, keepdims=True)
        return jnp.exp(s - m).astype(BF16), m

    def weighted_values(p, cls, j, hd):
        vw = window(v_ref, vp_ref if has_prev else None, cls, j)
        sel_kv = head0_kv if hd == 0 else jnp.logical_not(head0_kv)
        v_aug = jnp.where(sel_kv, vw, jnp.ones_like(vw))
        return jnp.dot(p, v_aug, preferred_element_type=F32)

    def finish(cls, j, pvs, maxes):
        rows = slice(j * ATTN_BLOCK, (j + 1) * ATTN_BLOCK)
        num = jnp.where(head0_q, pvs[0], pvs[1])
        den = pltpu.roll(jnp.where(head0_q, pvs[1], pvs[0]), HEAD_DIM, 1)
        o_ref[cls, rows, :] = num / den
        l_ref[cls, rows, :] = jnp.where(head0_q, maxes[0], maxes[1]) + jnp.log(den)

    s_vals, p_vals, m_vals, pv_vals = {}, {}, {}, {}
    for t in range(len(items) + ATTN_PV_LAG):
        if t < len(items):
            s_vals[t] = scores(*items[t])
        if 1 <= t <= len(items):
            p_vals[t - 1], m_vals[t - 1] = softmax_numerators(s_vals.pop(t - 1))
        if t >= ATTN_PV_LAG:
            i = t - ATTN_PV_LAG
            cls, j, hd = items[i]
            pv_vals[i] = weighted_values(p_vals.pop(i), cls, j, hd)
            if hd == 1:
                finish(cls, j, [pv_vals.pop(i - 1), pv_vals.pop(i)], [m_vals.pop(i - 1), m_vals.pop(i)])


def _attn_branch(qkv_cm, dilation):
    batch, _, n_slab, m, _ = qkv_cm.shape
    npair = n_slab // 3
    qb = min(ATTN_QUERIES_PER_STEP, m)
    ncls = min(ATTN_QUERIES_PER_STEP // qb, dilation)
    nsub = qb // ATTN_BLOCK
    nblk = m // qb
    has_prev = nblk > 1

    def own(off):
        return pl.BlockSpec((None, ncls, None, qb, HEAD_PAIR),
                            lambda b, c, p, n: (b, c, off * npair + p, n, 0))

    def prev(off):
        return pl.BlockSpec((None, ncls, None, ATTN_BLOCK, HEAD_PAIR),
                            lambda b, c, p, n: (b, c, off * npair + p, jnp.maximum(n * nsub - 1, 0), 0))

    out_spec = pl.BlockSpec((None, ncls, None, qb, HEAD_PAIR), lambda b, c, p, n: (b, c, p, n, 0))
    out_sds = jax.ShapeDtypeStruct((batch, dilation, npair, m, HEAD_PAIR), F32)
    in_specs = [own(0), own(1), own(2)] + ([prev(1), prev(2)] if has_prev else [])
    return pl.pallas_call(
        functools.partial(_attn_kernel, has_prev=has_prev),
        grid=(batch, dilation // ncls, npair, nblk),
        in_specs=in_specs,
        out_specs=[out_spec, out_spec],
        out_shape=[out_sds, out_sds],
        compiler_params=_params("parallel", "parallel", "parallel", "arbitrary"),
        name=f"dilated_attn_d{dilation}",
    )(*([qkv_cm] * len(in_specs)))


def _lru_kernel(x_ref, xp_ref, g_ref, cw_ref, cb_ref, w_ref, b_ref, lam_ref, ng_ref,
                o_ref, carry_ref, a_scr, u_scr, *, width):
    n = pl.program_id(1)

    @pl.when(n == 0)
    def _():
        carry_ref[...] = jnp.zeros_like(carry_ref)

    for b in range(x_ref.shape[0]):
        _lru_sequence(n, x_ref.at[b], xp_ref.at[b], g_ref.at[b], cw_ref, cb_ref, w_ref, b_ref, lam_ref, ng_ref,
                      o_ref.at[b], carry_ref.at[b], a_scr.at[b], u_scr.at[b], width)


def _lru_sequence(n, x_ref, xp_ref, g_ref, cw_ref, cb_ref, w_ref, b_ref, lam_ref, ng_ref,
                  o_ref, carry_ref, a_scr, u_scr, width):
    x = x_ref[...]
    rows = x.shape[0]
    xp = xp_ref[...] * jnp.where(n > 0, 1.0, 0.0).astype(F32)
    xe = jnp.concatenate([xp, x], axis=0)
    cw = cw_ref[...]
    xc = cw[LRU_CONV - 1:LRU_CONV, :] * x + cb_ref[...]
    for k in range(1, LRU_CONV):
        tap = LRU_CONV - 1 - k
        xc = xc + cw[tap:tap + 1, :] * _shift_rows(xe, k)[SUBLANES:, :]
    gates = jnp.dot(xc.astype(BF16), w_ref[...], preferred_element_type=F32) + b_ref[...]
    r = _sigmoid(gates[:, :width])
    i = _sigmoid(gates[:, width:])
    lam = lam_ref[...]
    softplus_neg = jnp.maximum(-lam, 0.0) + jnp.log1p(jnp.exp(-jnp.abs(lam)))
    log_a = (-LRU_C) * r * softplus_neg
    a = jnp.exp(log_a)
    t = jnp.tanh(log_a)
    u = jnp.sqrt(-2.0 * t / (1.0 - t)) * (i * xc)
    run = rows // SUBLANES
    pitch = _run_pitch(run)
    h_slabs = []
    for sl in range(width // LANES):
        cols = slice(sl * LANES, (sl + 1) * LANES)
        _store_runs(a_scr, sl, a[:, cols], run)
        _store_runs(u_scr, sl, u[:, cols], run)
        h_r = a_cum = None
        for r in range(run):
            step = _step_rows(r, run)
            a_r = a_scr[sl, step, :]
            u_r = u_scr[sl, step, :]
            if r == 0:
                h_r, a_cum = u_r, a_r
            else:
                h_r, a_cum = a_r * h_r + u_r, a_r * a_cum
                u_scr[sl, step, :] = h_r
                a_scr[sl, step, :] = a_cum
        p_inc, e_inc = a_cum, h_r
        for sh in (1, 2, 4):
            e_inc, p_inc = (p_inc * _sublane_delay(e_inc, sh, 0.0) + e_inc,
                            p_inc * _sublane_delay(p_inc, sh, 1.0))
        carry = carry_ref[:, cols]
        state_in = _sublane_delay(e_inc, 1, 0.0) + _sublane_delay(p_inc, 1, 1.0) * carry
        carry_ref[:, cols] = (e_inc + p_inc * carry)[SUBLANES - 1:SUBLANES, :]
        pieces = []
        for s in range(SUBLANES):
            blk = slice(s * pitch, s * pitch + run)
            pieces.append(u_scr[sl, blk, :] + a_scr[sl, blk, :] * state_in[s:s + 1, :])
        h_slabs.append(jnp.concatenate(pieces, axis=0))
    h = jnp.concatenate(h_slabs, axis=1)
    out = h * _gelu(g_ref[...])
    ms = jnp.mean(out * out, axis=-1, keepdims=True)
    o_ref[...] = (out * lax.rsqrt(ms + RMS_EPS) * ng_ref[...]).astype(BF16)


def _lru(rest, batch, seq, width, conv_w, conv_b, w_gate_bf, b_gate, lam, norm_g, chunk):
    rest_v = rest.reshape(batch, seq, rest.shape[-1])
    nchunk = seq // chunk
    per8 = chunk // SUBLANES

    def full(shape):
        return pl.BlockSpec(shape, lambda b, n: (0,) * len(shape))

    nb = min(SEQS_PER_STEP, batch)
    out = pl.pallas_call(
        functools.partial(_lru_kernel, width=width),
        grid=(batch // nb, nchunk),
        in_specs=[
            pl.BlockSpec((nb, chunk, width), lambda b, n: (b, n, 0)),
            pl.BlockSpec((nb, SUBLANES, width), lambda b, n: (b, jnp.maximum(n * per8 - 1, 0), 0)),
            pl.BlockSpec((nb, chunk, width), lambda b, n: (b, n, 1)),
            full((LRU_CONV, width)),
            full((1, width)),
            full((width, 2 * width)),
            full((1, 2 * width)),
            full((1, width)),
            full((1, width)),
        ],
        out_specs=pl.BlockSpec((nb, chunk, width), lambda b, n: (b, n, 0)),
        out_shape=jax.ShapeDtypeStruct((batch, seq, width), BF16),
        scratch_shapes=[pltpu.VMEM((nb, 1, width), F32)]
        + [pltpu.VMEM((nb, width // LANES, SUBLANES * _run_pitch(chunk // SUBLANES), LANES), F32)] * 2,
        compiler_params=_params("parallel", "arbitrary"),
        name="rg_lru",
    )(rest_v, rest_v, rest_v, conv_w, conv_b.reshape(1, width), w_gate_bf,
      b_gate.reshape(1, 2 * width), lam.reshape(1, width), norm_g.reshape(1, width))
    return out.reshape(batch * seq, width)


def _block_diag(w):
    h, i, j = w.shape
    eye = jnp.eye(h, dtype=w.dtype)
    return jnp.einsum('hij,hg->higj', w, eye).reshape(h * i, h * j)


def _cmul(ar, ai, br, bi):
    return ar * br - ai * bi, ar * bi + ai * br


def _s5_kernel(u_ref, bbd_ref, pwr_ref, pwi_ref, mur_ref, mui_ref, cbd_ref, d_ref, wg_ref, bg_ref, ng_ref,
               o_ref, cr_ref, ci_ref, *seq_scratch, nstate):
    n = pl.program_id(1)

    @pl.when(n == 0)
    def _():
        cr_ref[...] = jnp.zeros_like(cr_ref)
        ci_ref[...] = jnp.zeros_like(ci_ref)

    seqs = range(u_ref.shape[0])
    scratch = [seq_scratch[3 * b:3 * b + 3] for b in seqs]
    u_runs = [_s5_project(u_ref.at[b], bbd_ref, scratch[b][0], scratch[b][1]) for b in seqs]
    for b in seqs:
        _s5_scan(pwr_ref, pwi_ref, mur_ref, mui_ref, scratch[b][1], scratch[b][2], cr_ref.at[b], ci_ref.at[b], nstate)
    for b in seqs:
        _s5_output(u_runs[b], cbd_ref, d_ref, wg_ref, bg_ref, ng_ref, o_ref.at[b], scratch[b][0], scratch[b][2])


def _s5_project(u_ref, bbd_ref, perm_ref, bu_ref):
    u = u_ref[...]
    rows, width = u.shape
    run = rows // SUBLANES
    slabs = []
    for sl in range(width // LANES):
        _store_runs(perm_ref, sl, u[:, sl * LANES:(sl + 1) * LANES], run)
        slabs.append(jnp.concatenate([perm_ref[sl, _step_rows(r, run), :] for r in range(run)], axis=0))
    u_run = jnp.concatenate(slabs, axis=1)
    bu_ref[...] = jnp.dot(u_run.astype(BF16), bbd_ref[...], preferred_element_type=F32)
    return u_run


def _s5_scan(pwr_ref, pwi_ref, mur_ref, mui_ref, bu_ref, xb_ref, cr_ref, ci_ref, nstate):
    rows = bu_ref.shape[0]
    run = rows // SUBLANES
    for j in range(nstate // LANES):
        re_cols = slice(j * LANES, (j + 1) * LANES)
        im_cols = slice(nstate + j * LANES, nstate + (j + 1) * LANES)
        lr = pwr_ref[0:1, re_cols]
        li = pwi_ref[0:1, re_cols]
        xr = bu_ref[0:SUBLANES, re_cols]
        xi = bu_ref[0:SUBLANES, im_cols]
        for r in range(1, run):
            step = slice(r * SUBLANES, (r + 1) * SUBLANES)
            dr, di = _cmul(lr, li, xr, xi)
            xr = bu_ref[step, re_cols] + dr
            xi = bu_ref[step, im_cols] + di
            bu_ref[step, re_cols] = xr
            bu_ref[step, im_cols] = xi
        er, ei = xr, xi
        for sh in (1, 2, 4):
            k = run * sh - 1
            dr, di = _cmul(pwr_ref[k:k + 1, re_cols], pwi_ref[k:k + 1, re_cols],
                           _sublane_delay(er, sh, 0.0), _sublane_delay(ei, sh, 0.0))
            er, ei = er + dr, ei + di
        cr = cr_ref[:, re_cols]
        ci = ci_ref[:, re_cols]
        dr, di = _cmul(mur_ref[:, re_cols], mui_ref[:, re_cols], cr, ci)
        in_r = _sublane_delay(er, 1, 0.0) + dr
        in_i = _sublane_delay(ei, 1, 0.0) + di
        dr, di = _cmul(pwr_ref[rows - 1:rows, re_cols], pwi_ref[rows - 1:rows, re_cols], cr, ci)
        cr_ref[:, re_cols] = (er + dr)[SUBLANES - 1:SUBLANES, :]
        ci_ref[:, re_cols] = (ei + di)[SUBLANES - 1:SUBLANES, :]
        for r0 in range(0, run, 2):
            outs_r, outs_i = [], []
            for r in (r0, r0 + 1):
                step = slice(r * SUBLANES, (r + 1) * SUBLANES)
                dr, di = _cmul(pwr_ref[r:r + 1, re_cols], pwi_ref[r:r + 1, re_cols], in_r, in_i)
                outs_r.append(bu_ref[step, re_cols] + dr)
                outs_i.append(bu_ref[step, im_cols] + di)
            pair = slice(r0 * SUBLANES, (r0 + 2) * SUBLANES)
            xb_ref[pair, re_cols] = jnp.concatenate(outs_r, axis=0).astype(BF16)
            xb_ref[pair, im_cols] = jnp.concatenate(outs_i, axis=0).astype(BF16)


def _s5_output(u_run, cbd_ref, d_ref, wg_ref, bg_ref, ng_ref, o_ref, perm_ref, xb_ref):
    rows, width = u_run.shape
    run = rows // SUBLANES
    y = jnp.dot(xb_ref[...], cbd_ref[...], preferred_element_type=F32) + d_ref[...] * u_run
    y = _gelu(y)
    z = jnp.dot(y.astype(BF16), wg_ref[...], preferred_element_type=F32) + bg_ref[...]
    out = y * _sigmoid(z)
    ms = jnp.mean(out * out, axis=-1, keepdims=True)
    out = out * lax.rsqrt(ms + RMS_EPS) * ng_ref[...]
    slabs = []
    for sl in range(width // LANES):
        for r in range(run):
            perm_ref[sl, _step_rows(r, run), :] = out[r * SUBLANES:(r + 1) * SUBLANES, sl * LANES:(sl + 1) * LANES]
        slabs.append(_load_runs(perm_ref, sl, run))
    o_ref[...] = jnp.concatenate(slabs, axis=1).astype(BF16)


def _s5_tables(a_re, a_im, b_re, b_im, c_re, c_im, log_step, chunk):
    g, p = a_re.shape
    step = jnp.exp(log_step)[:, None]
    dt_re, dt_im = step * a_re, step * a_im
    mag = jnp.exp(dt_re)
    ab_re, ab_im = mag * jnp.cos(dt_im), mag * jnp.sin(dt_im)
    z_re, z_im = ab_re - 1.0, ab_im
    den = a_re * a_re + a_im * a_im
    f_re = (z_re * a_re + z_im * a_im) / den
    f_im = (z_im * a_re - z_re * a_im) / den
    bb_re = f_re[..., None] * b_re - f_im[..., None] * b_im
    bb_im = f_re[..., None] * b_im + f_im[..., None] * b_re
    eye = jnp.eye(g, dtype=F32)

    def in_map(bb):
        c = bb.shape[-1]
        return jnp.einsum('gpc,gh->gchp', bb, eye).reshape(g * c, g * p)

    def out_map(cc):
        c = cc.shape[1]
        return jnp.einsum('gcp,gh->gphc', cc, eye).reshape(g * p, g * c)

    bbd = jnp.concatenate([in_map(bb_re), in_map(bb_im)], axis=1)
    cbd = jnp.concatenate([out_map(c_re), out_map(-c_im)], axis=0)
    pr = ab_re.reshape(1, g * p)
    pi = ab_im.reshape(1, g * p)
    s = 1
    while s < chunk:
        lr, li = pr[s - 1:s], pi[s - 1:s]
        pr, pi = (jnp.concatenate([pr, pr * lr - pi * li], axis=0),
                  jnp.concatenate([pi, pr * li + pi * lr], axis=0))
        s *= 2
    run = chunk // SUBLANES
    run_rows = jnp.arange(1, SUBLANES) * run - 1
    mur = jnp.concatenate([jnp.ones((1, g * p), F32), pr[run_rows]], axis=0)
    mui = jnp.concatenate([jnp.zeros((1, g * p), F32), pi[run_rows]], axis=0)
    return bbd.astype(BF16), pr, pi, mur, mui, cbd.astype(BF16)


def _s5(rest, batch, seq, col_block, width, tables, d, w_glu_bf, b_glu, norm_g, chunk):
    bbd, pwr, pwi, mur, mui, cbd = tables
    nstate = pwr.shape[1]
    rest_v = rest.reshape(batch, seq, rest.shape[-1])

    def full(shape):
        return pl.BlockSpec(shape, lambda b, n: (0,) * len(shape))

    nb = min(SEQS_PER_STEP, batch)
    out = pl.pallas_call(
        functools.partial(_s5_kernel, nstate=nstate),
        grid=(batch // nb, seq // chunk),
        in_specs=[
            pl.BlockSpec((nb, chunk, width), lambda b, n: (b, n, col_block)),
            full((width, 2 * nstate)),
            full((chunk, nstate)),
            full((chunk, nstate)),
            full((SUBLANES, nstate)),
            full((SUBLANES, nstate)),
            full((2 * nstate, width)),
            full((1, width)),
            full((width, width)),
            full((1, width)),
            full((1, width)),
        ],
        out_specs=pl.BlockSpec((nb, chunk, width), lambda b, n: (b, n, 0)),
        out_shape=jax.ShapeDtypeStruct((batch, seq, width), BF16),
        scratch_shapes=[pltpu.VMEM((nb, 1, nstate), F32)] * 2 + [
            pltpu.VMEM((width // LANES, SUBLANES * _run_pitch(chunk // SUBLANES), LANES), F32),
            pltpu.VMEM((chunk, 2 * nstate), F32),
            pltpu.VMEM((chunk, 2 * nstate), BF16),
        ] * nb,
        compiler_params=_params("parallel", "arbitrary"),
        name="s5_scan",
    )(rest_v, bbd, pwr, pwi, mur, mui, cbd, d.reshape(1, width), w_glu_bf, b_glu.reshape(1, width),
      norm_g.reshape(1, width))
    return out.reshape(batch * seq, width)


def _layer_norm_rows(y, g, b):
    mu = jnp.mean(y, axis=-1, keepdims=True)
    yc = y - mu
    var = jnp.mean(yc * yc, axis=-1, keepdims=True)
    return yc * lax.rsqrt(var + LN_EPS) * g + b


def _mix_kernel(*refs, dilations):
    nb = len(dilations)
    o_refs, l_refs = refs[:nb], refs[nb:2 * nb]
    lru_ref, s5_ref, h_ref, w_ref, ga_ref, lg_ref, lb_ref, out_ref, tok_ref = refs[2 * nb:]
    npair = o_refs[0].shape[1]
    tm = h_ref.shape[0]

    def token_major(ref, d, slot, p):
        if d == 1:
            return ref[0, p]
        for c in range(d):
            tok_ref[slot, pl.ds(c, tm // d, stride=d), :] = ref[c, p]
        return tok_ref[slot]

    attn = []
    ssq = jnp.zeros((tm, 1), F32)
    for p in range(npair):
        outs = [token_major(o_refs[i], d, 2 * i, p) for i, d in enumerate(dilations)]
        lses = [token_major(l_refs[i], d, 2 * i + 1, p) for i, d in enumerate(dilations)]
        m = functools.reduce(jnp.maximum, lses)
        es = [jnp.exp(l - m) for l in lses]
        a = sum(e * o for e, o in zip(es, outs)) / sum(es)
        ssq = ssq + jnp.sum(a * a, axis=-1, keepdims=True)
        attn.append(a)
    scale = lax.rsqrt(ssq * (1.0 / (npair * HEAD_PAIR)) + RMS_EPS)
    ga = ga_ref[...]
    attn_n = [(a * scale * ga[:, p * HEAD_PAIR:(p + 1) * HEAD_PAIR]).astype(BF16) for p, a in enumerate(attn)]
    mixed = jnp.concatenate(attn_n + [lru_ref[...], s5_ref[...]], axis=1)
    mix = jnp.dot(mixed, w_ref[...], preferred_element_type=F32)
    y = DEEPNORM_ALPHA * h_ref[...] + mix
    out_ref[...] = _layer_norm_rows(y, lg_ref[...], lb_ref[...])


def _mix(attn_outs, dilations, lru, s5, h, seq, w_out_bf, g_attn, ln_g, ln_b, tm):
    t, d = h.shape
    aw = lru.shape[1]
    sw = s5.shape[1]
    nseq = seq // tm
    npair = aw // HEAD_PAIR

    def rows(w):
        return pl.BlockSpec((tm, w), lambda i: (i, 0))

    def full(shape):
        return pl.BlockSpec(shape, lambda i: (0,) * len(shape))

    def class_major(dil):
        return pl.BlockSpec((None, dil, npair, tm // dil, HEAD_PAIR),
                            lambda i: (i // nseq, 0, 0, i % nseq, 0))

    branch_specs = [class_major(dil) for dil in dilations]
    return pl.pallas_call(
        functools.partial(_mix_kernel, dilations=tuple(dilations)),
        grid=(t // tm,),
        in_specs=branch_specs + branch_specs + [rows(aw), rows(sw), rows(d), full((d, d)),
                                                full((1, aw)), full((1, d)), full((1, d))],
        out_specs=rows(d),
        out_shape=jax.ShapeDtypeStruct((t, d), F32),
        scratch_shapes=[pltpu.VMEM((2 * len(dilations), tm, HEAD_PAIR), F32)],
        compiler_params=_params("parallel"),
        name="mix_out_ln",
    )(*[o for o, _ in attn_outs], *[l for _, l in attn_outs], lru, s5, h, w_out_bf,
      g_attn.reshape(1, aw), ln_g.reshape(1, d), ln_b.reshape(1, d))


def _ffn_kernel(h_ref, hp_ref, wu_ref, cw_ref, cb_ref, wd_ref, lg_ref, lb_ref, out_ref,
                *, tiles_per_seq, ff_chunk):
    i = pl.program_id(0)
    keep_prev = jnp.where(i % tiles_per_seq == 0, 0.0, 1.0).astype(F32)
    h = h_ref[...]
    he = jnp.concatenate([hp_ref[...] * keep_prev, h], axis=0).astype(BF16)
    d_ff = wd_ref.shape[0]

    def conv_branch(col0):
        cols = slice(col0, col0 + ff_chunk)
        up = jnp.dot(he, wu_ref[:, cols], preferred_element_type=F32)
        cw = cw_ref[:, cols]
        y = cw[FFN_CONV - 1:FFN_CONV, :] * up
        for k in range(1, FFN_CONV):
            tap = FFN_CONV - 1 - k
            y = y + cw[tap:tap + 1, :] * _shift_rows(up, k)
        return y[SUBLANES:, :] + cb_ref[:, cols]

    acc = None
    for j in range(d_ff // ff_chunk):
        gate = conv_branch(j * ff_chunk)
        val = conv_branch(d_ff + j * ff_chunk)
        act = (_gelu(gate) * val).astype(BF16)
        part = jnp.dot(act, wd_ref[j * ff_chunk:(j + 1) * ff_chunk, :], preferred_element_type=F32)
        acc = part if acc is None else acc + part
    y = DEEPNORM_ALPHA * h + acc
    out_ref[...] = _layer_norm_rows(y, lg_ref[...], lb_ref[...])


def _ffn(h, seq, w_up_bf, conv_w, conv_b, w_down_bf, ln_g, ln_b, tm, cw):
    t, d = h.shape
    d_ff = w_down_bf.shape[0]
    per8 = tm // SUBLANES

    def resident(shape):
        return pl.BlockSpec(shape, lambda i: (0,) * len(shape), pipeline_mode=pl.Buffered(1))

    return pl.pallas_call(
        functools.partial(_ffn_kernel, tiles_per_seq=seq // tm, ff_chunk=cw),
        grid=(t // tm,),
        in_specs=[
            pl.BlockSpec((tm, d), lambda i: (i, 0)),
            pl.BlockSpec((SUBLANES, d), lambda i: (jnp.maximum(i * per8 - 1, 0), 0)),
            resident((d, 2 * d_ff)),
            resident((FFN_CONV, 2 * d_ff)),
            resident((1, 2 * d_ff)),
            resident((d_ff, d)),
            resident((1, d)),
            resident((1, d)),
        ],
        out_specs=pl.BlockSpec((tm, d), lambda i: (i, 0)),
        out_shape=jax.ShapeDtypeStruct((t, d), F32),
        compiler_params=pltpu.CompilerParams(dimension_semantics=("parallel",),
                                             vmem_limit_bytes=FFN_VMEM_LIMIT),
        name="conv_ffn_ln",
    )(h, h, w_up_bf, conv_w, conv_b.reshape(1, 2 * d_ff), w_down_bf,
      ln_g.reshape(1, d), ln_b.reshape(1, d))


def kernel(x, w_in, lru_conv_w, lru_conv_b, lru_wr, lru_br, lru_wi, lru_bi, lru_lambda, s5_a_re, s5_a_im, s5_b_re, s5_b_im, s5_c_re, s5_c_im, s5_d, s5_log_step, s5_w_glu, s5_b_glu, mix_norm_g, w_out, ln1_g, ln1_b, w_up, ffn_conv_w, ffn_conv_b, w_down, ln2_g, ln2_b):
    batch, seq, d_model = x.shape
    depth = w_in.shape[0]
    lru_width = lru_conv_w.shape[-1]
    s5_width = s5_w_glu.shape[-1]
    attn_width = (w_in.shape[-1] - 2 * lru_width - s5_width) // 3
    assert attn_width % HEAD_PAIR == 0 and lru_width == attn_width
    assert all(w // d == ATTN_BLOCK for w, d in DILATED_PAIRS)
    dilations = [d for _, d in DILATED_PAIRS]
    s5_chunk = 256
    lru_chunk = 256
    row_tile = 512
    proj_tile = 1024
    ffn_tile = 512
    ff_chunk = 1536
    s5_col_block = (2 * lru_width) // s5_width
    assert s5_col_block * s5_width == 2 * lru_width

    cos_t, sin_t = _rope_tables(seq)
    h = x.reshape(batch * seq, d_model)
    for l in range(depth):
        qkvs, rest = _proj(h, w_in[l].astype(BF16), cos_t, sin_t, batch, attn_width, dilations, proj_tile)
        attn_outs = [_attn_branch(qkv_cm, d) for qkv_cm, d in zip(qkvs, dilations)]
        w_gate = jnp.concatenate([_block_diag(lru_wr[l]), _block_diag(lru_wi[l])], axis=1).astype(BF16)
        b_gate = jnp.concatenate([lru_br[l], lru_bi[l]])
        lru = _lru(rest, batch, seq, lru_width, lru_conv_w[l], lru_conv_b[l], w_gate, b_gate,
                   lru_lambda[l], mix_norm_g[l, attn_width:attn_width + lru_width], lru_chunk)
        tables = _s5_tables(s5_a_re[l], s5_a_im[l], s5_b_re[l], s5_b_im[l], s5_c_re[l], s5_c_im[l],
                            s5_log_step[l], s5_chunk)
        s5 = _s5(rest, batch, seq, s5_col_block, s5_width, tables, s5_d[l].reshape(-1),
                 s5_w_glu[l].astype(BF16), s5_b_glu[l], mix_norm_g[l, attn_width + lru_width:], s5_chunk)
        h1 = _mix(attn_outs, dilations, lru, s5, h, seq, w_out[l].astype(BF16), mix_norm_g[l, :attn_width],
                  ln1_g[l], ln1_b[l], row_tile)
        h = _ffn(h1, seq, w_up[l].astype(BF16), ffn_conv_w[l], ffn_conv_b[l], w_down[l].astype(BF16),
                 ln2_g[l], ln2_b[l], ffn_tile, ff_chunk)
    return h.reshape(batch, seq, d_model)
```

```python
import functools

import jax
import jax.numpy as jnp
from jax import lax
from jax.experimental import pallas as pl
from jax.experimental.pallas import tpu as pltpu

F32 = jnp.float32
BF16 = jnp.bfloat16

HEAD_DIM = 64
HEAD_PAIR = 2 * HEAD_DIM
DILATED_PAIRS = ((128, 1), (512, 4), (2048, 16))
ATTN_BLOCK = 128
ATTN_QUERIES_PER_STEP = 1024
SEQS_PER_STEP = 2
S5_SEQS_PER_STEP = 4
MIX_ROW_PARTS = 2
ATTN_PV_LAG = 2
ROPE_THETA = 10000.0
LRU_HEAD = 64
LRU_CONV = 4
LRU_C = 8.0
S5_GROUP = 16
S5_STATE = 64
FFN_CONV = 3
DEPTH = 2
DEEPNORM_ALPHA = (2 * DEPTH) ** 0.25
LN_EPS = 1e-5
RMS_EPS = 1e-6
NEG_BIG = -1e30
LN2 = 0.6931471805599453
Q_SCALE = HEAD_DIM ** -0.5 / LN2

LANES = 128
SUBLANES = 8
VMEM_LIMIT = 48 * 1024 * 1024
FFN_VMEM_LIMIT = 56 * 1024 * 1024


def _params(*sem):
    return pltpu.CompilerParams(dimension_semantics=sem, vmem_limit_bytes=VMEM_LIMIT)


def _gelu(x):
    return jax.nn.gelu(x)


def _sigmoid(x):
    return 0.5 * jnp.tanh(0.5 * x) + 0.5


def _shift_rows(x, s):
    return pltpu.roll(x, s, 0)


def _run_pitch(run):
    return run + SUBLANES


def _store_runs(scr_ref, slab, x, run):
    pitch = _run_pitch(run)
    for s in range(SUBLANES):
        scr_ref[slab, s * pitch:s * pitch + run, :] = x[s * run:(s + 1) * run, :]


def _load_runs(scr_ref, slab, run):
    pitch = _run_pitch(run)
    return jnp.concatenate([scr_ref[slab, s * pitch:s * pitch + run, :] for s in range(SUBLANES)], axis=0)


def _step_rows(r, run):
    return pl.ds(r, SUBLANES, stride=_run_pitch(run))


def _sublane_delay(v, s, fill):
    sub = lax.broadcasted_iota(jnp.int32, v.shape, 0)
    return jnp.where(sub >= s, pltpu.roll(v, s, 0), fill)


def _proj_kernel(h_ref, w_ref, cos_ref, sin_ref, *refs, attn_width, dilations):
    out_refs = refs[:len(dilations)]
    rest_ref, stage_ref = refs[len(dilations):]
    n_slab = 3 * attn_width // LANES
    n_rot = 2 * attn_width // LANES
    rows = h_ref.shape[0]
    acc = jnp.dot(h_ref[...].astype(BF16), w_ref[...], preferred_element_type=F32)
    cos = cos_ref[...]
    sin = sin_ref[...]
    lane = lax.broadcasted_iota(jnp.int32, cos.shape, 1)
    first_half = (lane & (HEAD_DIM // 2)) == 0
    for cb in range(n_slab):
        x = acc[:, cb * LANES:(cb + 1) * LANES]
        if cb < n_rot:
            partner = jnp.where(first_half, pltpu.roll(x, LANES - HEAD_DIM // 2, 1),
                                pltpu.roll(x, HEAD_DIM // 2, 1))
            x = x * cos + partner * sin
        if cb < n_rot // 2:
            x = x * Q_SCALE
        stage_ref[cb] = x
    rest_ref[...] = acc[:, 3 * attn_width:]
    for out_ref, d in zip(out_refs, dilations):
        for c in range(d):
            for cb in range(n_slab):
                if d == 1:
                    piece = stage_ref[cb]
                else:
                    piece = stage_ref[cb, pl.ds(c, rows // d, stride=d), :]
                out_ref[c, cb] = piece.astype(BF16)


def _rope_tables(seq):
    half = HEAD_DIM // 2
    pos = jnp.arange(seq, dtype=F32)
    inv = ROPE_THETA ** (-jnp.arange(half, dtype=F32) * 2.0 / HEAD_DIM)
    ang = pos[:, None] * inv[None, :]
    cos = jnp.cos(ang)
    sin = jnp.sin(ang)
    cos_t = jnp.concatenate([cos, cos, cos, cos], axis=1)
    sin_t = jnp.concatenate([-sin, sin, -sin, sin], axis=1)
    return cos_t, sin_t


def _proj(h, w_in_bf, cos_t, sin_t, batch, attn_width, dilations, tm):
    t, d = h.shape
    d_in = w_in_bf.shape[1]
    seq = cos_t.shape[0]
    nseq = seq // tm
    rest_w = d_in - 3 * attn_width
    n_slab = 3 * attn_width // LANES
    qkv_specs = [pl.BlockSpec((None, dil, n_slab, tm // dil, LANES),
                              lambda i: (i // nseq, 0, 0, i % nseq, 0)) for dil in dilations]
    qkv_shapes = [jax.ShapeDtypeStruct((batch, dil, n_slab, seq // dil, LANES), BF16) for dil in dilations]
    outs = pl.pallas_call(
        functools.partial(_proj_kernel, attn_width=attn_width, dilations=tuple(dilations)),
        grid=(t // tm,),
        in_specs=[
            pl.BlockSpec((tm, d), lambda i: (i, 0)),
            pl.BlockSpec((d, d_in), lambda i: (0, 0)),
            pl.BlockSpec((tm, LANES), lambda i: (i % nseq, 0)),
            pl.BlockSpec((tm, LANES), lambda i: (i % nseq, 0)),
        ],
        out_specs=qkv_specs + [pl.BlockSpec((tm, rest_w), lambda i: (i, 0))],
        out_shape=qkv_shapes + [jax.ShapeDtypeStruct((t, rest_w), F32)],
        scratch_shapes=[pltpu.VMEM((n_slab, tm, LANES), F32)],
        compiler_params=_params("parallel"),
        name="proj_rope",
    )(h, w_in_bf, cos_t, sin_t)
    return outs[:-1], outs[-1]


def _attn_kernel(q_ref, k_ref, v_ref, *rest, has_prev):
    if has_prev:
        kp_ref, vp_ref, o_ref, l_ref = rest
    else:
        o_ref, l_ref = rest
    ncls, qb, _ = q_ref.shape
    win = (ATTN_BLOCK, 2 * ATTN_BLOCK)
    row = lax.broadcasted_iota(jnp.int32, win, 0)
    col = lax.broadcasted_iota(jnp.int32, win, 1)
    band_bias = jnp.where((col >= row) & (col <= row + ATTN_BLOCK), 0.0, NEG_BIG).astype(F32)
    if has_prev:
        no_prev = jnp.where(pl.program_id(3) > 0, 0.0, NEG_BIG).astype(F32)
        first_bias = band_bias + jnp.where(col < ATTN_BLOCK, no_prev, 0.0)
    else:
        first_bias = jnp.where(col < ATTN_BLOCK, NEG_BIG, band_bias).astype(F32)
    head0_q = lax.broadcasted_iota(jnp.int32, (ATTN_BLOCK, HEAD_PAIR), 1) < HEAD_DIM
    head0_kv = lax.broadcasted_iota(jnp.int32, (2 * ATTN_BLOCK, HEAD_PAIR), 1) < HEAD_DIM
    contract_last = (((1,), (1,)), ((), ()))
    items = [(cls, j, hd) for cls in range(ncls) for j in range(qb // ATTN_BLOCK) for hd in range(2)]

    def window(ref, prev_ref, cls, j):
        rows = slice(j * ATTN_BLOCK, (j + 1) * ATTN_BLOCK)
        if j > 0:
            return ref[cls, (j - 1) * ATTN_BLOCK:(j + 1) * ATTN_BLOCK, :]
        if has_prev:
            return jnp.concatenate([prev_ref[cls], ref[cls, rows, :]], axis=0)
        return jnp.concatenate([ref[cls, rows, :]] * 2, axis=0)

    def scores(cls, j, hd):
        q = q_ref[cls, j * ATTN_BLOCK:(j + 1) * ATTN_BLOCK, :]
        sel_q = head0_q if hd == 0 else jnp.logical_not(head0_q)
        qh = jnp.where(sel_q, q, jnp.zeros_like(q))
        kw = window(k_ref, kp_ref if has_prev else None, cls, j)
        bias = band_bias if j > 0 else first_bias
        return lax.dot_general(qh, kw, contract_last, preferred_element_type=F32) + bias

    def softmax_numerators(s):
        m = jnp.max(s, axis=1, keepdims=True)
        return jnp.exp2(s - m).astype(BF16), m

    def weighted_values(p, cls, j, hd):
        vw = window(v_ref, vp_ref if has_prev else None, cls, j)
        sel_kv = head0_kv if hd == 0 else jnp.logical_not(head0_kv)
        v_aug = jnp.where(sel_kv, vw, jnp.ones_like(vw))
        return jnp.dot(p, v_aug, preferred_element_type=F32)

    def finish(cls, j, pvs, maxes):
        rows = slice(j * ATTN_BLOCK, (j + 1) * ATTN_BLOCK)
        num = jnp.where(head0_q, pvs[0], pvs[1])
        den = pltpu.roll(jnp.where(head0_q, pvs[1], pvs[0]), HEAD_DIM, 1)
        o_ref[cls, rows, :] = (num / den).astype(o_ref.dtype)
        l_ref[cls, rows, :] = (jnp.where(head0_q, maxes[0], maxes[1]) + jnp.log2(den)) * LN2

    s_vals, p_vals, m_vals, pv_vals = {}, {}, {}, {}
    for t in range(len(items) + ATTN_PV_LAG):
        if t < len(items):
            s_vals[t] = scores(*items[t])
        if 1 <= t <= len(items):
            p_vals[t - 1], m_vals[t - 1] = softmax_numerators(s_vals.pop(t - 1))
        if t >= ATTN_PV_LAG:
            i = t - ATTN_PV_LAG
            cls, j, hd = items[i]
            pv_vals[i] = weighted_values(p_vals.pop(i), cls, j, hd)
            if hd == 1:
                finish(cls, j, [pv_vals.pop(i - 1), pv_vals.pop(i)], [m_vals.pop(i - 1), m_vals.pop(i)])


def _attn_branch(qkv_cm, dilation):
    batch, _, n_slab, m, _ = qkv_cm.shape
    npair = n_slab // 3
    qb = min(ATTN_QUERIES_PER_STEP, m)
    ncls = min(ATTN_QUERIES_PER_STEP // qb, dilation)
    nsub = qb // ATTN_BLOCK
    nblk = m // qb
    has_prev = nblk > 1

    def own(off):
        return pl.BlockSpec((None, ncls, None, qb, HEAD_PAIR),
                            lambda b, c, p, n: (b, c, off * npair + p, n, 0))

    def prev(off):
        return pl.BlockSpec((None, ncls, None, ATTN_BLOCK, HEAD_PAIR),
                            lambda b, c, p, n: (b, c, off * npair + p, jnp.maximum(n * nsub - 1, 0), 0))

    out_spec = pl.BlockSpec((None, ncls, None, qb, HEAD_PAIR), lambda b, c, p, n: (b, c, p, n, 0))
    out_shapes = [jax.ShapeDtypeStruct((batch, dilation, npair, m, HEAD_PAIR), dt) for dt in (BF16, F32)]
    in_specs = [own(0), own(1), own(2)] + ([prev(1), prev(2)] if has_prev else [])
    return pl.pallas_call(
        functools.partial(_attn_kernel, has_prev=has_prev),
        grid=(batch, dilation // ncls, npair, nblk),
        in_specs=in_specs,
        out_specs=[out_spec, out_spec],
        out_shape=out_shapes,
        compiler_params=_params("parallel", "parallel", "parallel", "arbitrary"),
        name=f"dilated_attn_d{dilation}",
    )(*([qkv_cm] * len(in_specs)))


def _lru_kernel(x_ref, xp_ref, g_ref, cw_ref, cb_ref, w_ref, b_ref, lam_ref, ng_ref,
                o_ref, carry_ref, a_scr, u_scr, *, width):
    n = pl.program_id(1)

    @pl.when(n == 0)
    def _():
        carry_ref[...] = jnp.zeros_like(carry_ref)

    for b in range(x_ref.shape[0]):
        _lru_sequence(n, x_ref.at[b], xp_ref.at[b], g_ref.at[b], cw_ref, cb_ref, w_ref, b_ref, lam_ref, ng_ref,
                      o_ref.at[b], carry_ref.at[b], a_scr.at[b], u_scr.at[b], width)


def _lru_sequence(n, x_ref, xp_ref, g_ref, cw_ref, cb_ref, w_ref, b_ref, lam_ref, ng_ref,
                  o_ref, carry_ref, a_scr, u_scr, width):
    x = x_ref[...]
    rows = x.shape[0]
    xp = xp_ref[...] * jnp.where(n > 0, 1.0, 0.0).astype(F32)
    xe = jnp.concatenate([xp, x], axis=0)
    cw = cw_ref[...]
    xc = cw[LRU_CONV - 1:LRU_CONV, :] * x + cb_ref[...]
    for k in range(1, LRU_CONV):
        tap = LRU_CONV - 1 - k
        xc = xc + cw[tap:tap + 1, :] * _shift_rows(xe, k)[SUBLANES:, :]
    gates = jnp.dot(xc.astype(BF16), w_ref[...], preferred_element_type=F32) + b_ref[...]
    r = _sigmoid(gates[:, :width])
    i = _sigmoid(gates[:, width:])
    lam = lam_ref[...]
    softplus_neg = jnp.maximum(-lam, 0.0) + jnp.log1p(jnp.exp(-jnp.abs(lam)))
    log_a = (-LRU_C) * r * softplus_neg
    a = jnp.exp(log_a)
    t = jnp.tanh(log_a)
    u = jnp.sqrt(-2.0 * t / (1.0 - t)) * (i * xc)
    run = rows // SUBLANES
    pitch = _run_pitch(run)
    h_slabs = []
    for sl in range(width // LANES):
        cols = slice(sl * LANES, (sl + 1) * LANES)
        _store_runs(a_scr, sl, a[:, cols], run)
        _store_runs(u_scr, sl, u[:, cols], run)
        h_r = a_cum = None
        for r in range(run):
            step = _step_rows(r, run)
            a_r = a_scr[sl, step, :]
            u_r = u_scr[sl, step, :]
            if r == 0:
                h_r, a_cum = u_r, a_r
            else:
                h_r, a_cum = a_r * h_r + u_r, a_r * a_cum
                u_scr[sl, step, :] = h_r
                a_scr[sl, step, :] = a_cum
        p_inc, e_inc = a_cum, h_r
        for sh in (1, 2, 4):
            e_inc, p_inc = (p_inc * _sublane_delay(e_inc, sh, 0.0) + e_inc,
                            p_inc * _sublane_delay(p_inc, sh, 1.0))
        carry = carry_ref[:, cols]
        state_in = _sublane_delay(e_inc, 1, 0.0) + _sublane_delay(p_inc, 1, 1.0) * carry
        carry_ref[:, cols] = (e_inc + p_inc * carry)[SUBLANES - 1:SUBLANES, :]
        pieces = []
        for s in range(SUBLANES):
            blk = slice(s * pitch, s * pitch + run)
            pieces.append(u_scr[sl, blk, :] + a_scr[sl, blk, :] * state_in[s:s + 1, :])
        h_slabs.append(jnp.concatenate(pieces, axis=0))
    h = jnp.concatenate(h_slabs, axis=1)
    out = h * _gelu(g_ref[...])
    ms = jnp.mean(out * out, axis=-1, keepdims=True)
    o_ref[...] = (out * lax.rsqrt(ms + RMS_EPS) * ng_ref[...]).astype(BF16)


def _lru(rest, batch, seq, width, conv_w, conv_b, w_gate_bf, b_gate, lam, norm_g, chunk):
    rest_v = rest.reshape(batch, seq, rest.shape[-1])
    nchunk = seq // chunk
    per8 = chunk // SUBLANES

    def full(shape):
        return pl.BlockSpec(shape, lambda b, n: (0,) * len(shape))

    nb = min(SEQS_PER_STEP, batch)
    out = pl.pallas_call(
        functools.partial(_lru_kernel, width=width),
        grid=(batch // nb, nchunk),
        in_specs=[
            pl.BlockSpec((nb, chunk, width), lambda b, n: (b, n, 0)),
            pl.BlockSpec((nb, SUBLANES, width), lambda b, n: (b, jnp.maximum(n * per8 - 1, 0), 0)),
            pl.BlockSpec((nb, chunk, width), lambda b, n: (b, n, 1)),
            full((LRU_CONV, width)),
            full((1, width)),
            full((width, 2 * width)),
            full((1, 2 * width)),
            full((1, width)),
            full((1, width)),
        ],
        out_specs=pl.BlockSpec((nb, chunk, width), lambda b, n: (b, n, 0)),
        out_shape=jax.ShapeDtypeStruct((batch, seq, width), BF16),
        scratch_shapes=[pltpu.VMEM((nb, 1, width), F32)]
        + [pltpu.VMEM((nb, width // LANES, SUBLANES * _run_pitch(chunk // SUBLANES), LANES), F32)] * 2,
        compiler_params=_params("parallel", "arbitrary"),
        name="rg_lru",
    )(rest_v, rest_v, rest_v, conv_w, conv_b.reshape(1, width), w_gate_bf,
      b_gate.reshape(1, 2 * width), lam.reshape(1, width), norm_g.reshape(1, width))
    return out.reshape(batch * seq, width)


def _block_diag(w):
    h, i, j = w.shape
    eye = jnp.eye(h, dtype=w.dtype)
    return jnp.einsum('hij,hg->higj', w, eye).reshape(h * i, h * j)


def _cmul(ar, ai, br, bi):
    return ar * br - ai * bi, ar * bi + ai * br


def _s5_kernel(u_ref, bbd_ref, pwr_ref, pwi_ref, mur_ref, mui_ref, cbd_ref, d_ref, wg_ref, bg_ref, ng_ref,
               o_ref, cr_ref, ci_ref, perm_ref, bu_ref, xb_ref, *, nstate):
    n = pl.program_id(1)

    @pl.when(n == 0)
    def _():
        cr_ref[...] = jnp.zeros_like(cr_ref)
        ci_ref[...] = jnp.zeros_like(ci_ref)

    nb, rows, width = u_ref.shape
    run = rows // SUBLANES
    u_run = jnp.concatenate([_s5_to_runs(u_ref.at[b], perm_ref.at[b]) for b in range(nb)], axis=0)
    bu = jnp.dot(u_run.astype(BF16), bbd_ref[...], preferred_element_type=F32)
    bu_ref[...] = bu.reshape(nb, rows, bu.shape[-1])
    for b in range(nb):
        _s5_scan(pwr_ref, pwi_ref, mur_ref, mui_ref, bu_ref.at[b], xb_ref.at[b], cr_ref.at[b], ci_ref.at[b], nstate)
    xb = xb_ref[...].reshape(nb * rows, xb_ref.shape[-1])
    y = jnp.dot(xb, cbd_ref[...], preferred_element_type=F32) + d_ref[...] * u_run
    y = _gelu(y)
    z = jnp.dot(y.astype(BF16), wg_ref[...], preferred_element_type=F32) + bg_ref[...]
    out = y * _sigmoid(z)
    ms = jnp.mean(out * out, axis=-1, keepdims=True)
    out = out * lax.rsqrt(ms + RMS_EPS) * ng_ref[...]
    for b in range(nb):
        slabs = []
        for sl in range(width // LANES):
            for r in range(run):
                step = slice(b * rows + r * SUBLANES, b * rows + (r + 1) * SUBLANES)
                perm_ref[b, sl, _step_rows(r, run), :] = out[step, sl * LANES:(sl + 1) * LANES]
            slabs.append(_load_runs(perm_ref.at[b], sl, run))
        o_ref[b] = jnp.concatenate(slabs, axis=1).astype(BF16)


def _s5_to_runs(u_ref, perm_ref):
    u = u_ref[...]
    rows, width = u.shape
    run = rows // SUBLANES
    slabs = []
    for sl in range(width // LANES):
        _store_runs(perm_ref, sl, u[:, sl * LANES:(sl + 1) * LANES], run)
        slabs.append(jnp.concatenate([perm_ref[sl, _step_rows(r, run), :] for r in range(run)], axis=0))
    return jnp.concatenate(slabs, axis=1)


def _s5_scan(pwr_ref, pwi_ref, mur_ref, mui_ref, bu_ref, xb_ref, cr_ref, ci_ref, nstate):
    rows = bu_ref.shape[0]
    run = rows // SUBLANES
    for j in range(nstate // LANES):
        re_cols = slice(j * LANES, (j + 1) * LANES)
        im_cols = slice(nstate + j * LANES, nstate + (j + 1) * LANES)
        lr = pwr_ref[0:1, re_cols]
        li = pwi_ref[0:1, re_cols]
        xr = bu_ref[0:SUBLANES, re_cols]
        xi = bu_ref[0:SUBLANES, im_cols]
        for r in range(1, run):
            step = slice(r * SUBLANES, (r + 1) * SUBLANES)
            dr, di = _cmul(lr, li, xr, xi)
            xr = bu_ref[step, re_cols] + dr
            xi = bu_ref[step, im_cols] + di
            bu_ref[step, re_cols] = xr
            bu_ref[step, im_cols] = xi
        er, ei = xr, xi
        for sh in (1, 2, 4):
            k = run * sh - 1
            dr, di = _cmul(pwr_ref[k:k + 1, re_cols], pwi_ref[k:k + 1, re_cols],
                           _sublane_delay(er, sh, 0.0), _sublane_delay(ei, sh, 0.0))
            er, ei = er + dr, ei + di
        cr = cr_ref[:, re_cols]
        ci = ci_ref[:, re_cols]
        dr, di = _cmul(mur_ref[:, re_cols], mui_ref[:, re_cols], cr, ci)
        in_r = _sublane_delay(er, 1, 0.0) + dr
        in_i = _sublane_delay(ei, 1, 0.0) + di
        dr, di = _cmul(pwr_ref[rows - 1:rows, re_cols], pwi_ref[rows - 1:rows, re_cols], cr, ci)
        cr_ref[:, re_cols] = (er + dr)[SUBLANES - 1:SUBLANES, :]
        ci_ref[:, re_cols] = (ei + di)[SUBLANES - 1:SUBLANES, :]
        for r0 in range(0, run, 2):
            outs_r, outs_i = [], []
            for r in (r0, r0 + 1):
                step = slice(r * SUBLANES, (r + 1) * SUBLANES)
                dr, di = _cmul(pwr_ref[r:r + 1, re_cols], pwi_ref[r:r + 1, re_cols], in_r, in_i)
                outs_r.append(bu_ref[step, re_cols] + dr)
                outs_i.append(bu_ref[step, im_cols] + di)
            pair = slice(r0 * SUBLANES, (r0 + 2) * SUBLANES)
            xb_ref[pair, re_cols] = jnp.concatenate(outs_r, axis=0).astype(BF16)
            xb_ref[pair, im_cols] = jnp.concatenate(outs_i, axis=0).astype(BF16)


def _s5_tables(a_re, a_im, b_re, b_im, c_re, c_im, log_step, chunk):
    g, p = a_re.shape
    step = jnp.exp(log_step)[:, None]
    dt_re, dt_im = step * a_re, step * a_im
    mag = jnp.exp(dt_re)
    ab_re, ab_im = mag * jnp.cos(dt_im), mag * jnp.sin(dt_im)
    z_re, z_im = ab_re - 1.0, ab_im
    den = a_re * a_re + a_im * a_im
    f_re = (z_re * a_re + z_im * a_im) / den
    f_im = (z_im * a_re - z_re * a_im) / den
    bb_re = f_re[..., None] * b_re - f_im[..., None] * b_im
    bb_im = f_re[..., None] * b_im + f_im[..., None] * b_re
    eye = jnp.eye(g, dtype=F32)

    def in_map(bb):
        c = bb.shape[-1]
        return jnp.einsum('gpc,gh->gchp', bb, eye).reshape(g * c, g * p)

    def out_map(cc):
        c = cc.shape[1]
        return jnp.einsum('gcp,gh->gphc', cc, eye).reshape(g * p, g * c)

    bbd = jnp.concatenate([in_map(bb_re), in_map(bb_im)], axis=1)
    cbd = jnp.concatenate([out_map(c_re), out_map(-c_im)], axis=0)
    pr = ab_re.reshape(1, g * p)
    pi = ab_im.reshape(1, g * p)
    s = 1
    while s < chunk:
        lr, li = pr[s - 1:s], pi[s - 1:s]
        pr, pi = (jnp.concatenate([pr, pr * lr - pi * li], axis=0),
                  jnp.concatenate([pi, pr * li + pi * lr], axis=0))
        s *= 2
    run = chunk // SUBLANES
    run_rows = jnp.arange(1, SUBLANES) * run - 1
    mur = jnp.concatenate([jnp.ones((1, g * p), F32), pr[run_rows]], axis=0)
    mui = jnp.concatenate([jnp.zeros((1, g * p), F32), pi[run_rows]], axis=0)
    return bbd.astype(BF16), pr, pi, mur, mui, cbd.astype(BF16)


def _s5(rest, batch, seq, col_block, width, tables, d, w_glu_bf, b_glu, norm_g, chunk):
    bbd, pwr, pwi, mur, mui, cbd = tables
    nstate = pwr.shape[1]
    rest_v = rest.reshape(batch, seq, rest.shape[-1])

    def full(shape):
        return pl.BlockSpec(shape, lambda b, n: (0,) * len(shape))

    nb = min(S5_SEQS_PER_STEP, batch)
    out = pl.pallas_call(
        functools.partial(_s5_kernel, nstate=nstate),
        grid=(batch // nb, seq // chunk),
        in_specs=[
            pl.BlockSpec((nb, chunk, width), lambda b, n: (b, n, col_block)),
            full((width, 2 * nstate)),
            full((chunk, nstate)),
            full((chunk, nstate)),
            full((SUBLANES, nstate)),
            full((SUBLANES, nstate)),
            full((2 * nstate, width)),
            full((1, width)),
            full((width, width)),
            full((1, width)),
            full((1, width)),
        ],
        out_specs=pl.BlockSpec((nb, chunk, width), lambda b, n: (b, n, 0)),
        out_shape=jax.ShapeDtypeStruct((batch, seq, width), BF16),
        scratch_shapes=[pltpu.VMEM((nb, 1, nstate), F32)] * 2 + [
            pltpu.VMEM((nb, width // LANES, SUBLANES * _run_pitch(chunk // SUBLANES), LANES), F32),
            pltpu.VMEM((nb, chunk, 2 * nstate), F32),
            pltpu.VMEM((nb, chunk, 2 * nstate), BF16),
        ],
        compiler_params=_params("parallel", "arbitrary"),
        name="s5_scan",
    )(rest_v, bbd, pwr, pwi, mur, mui, cbd, d.reshape(1, width), w_glu_bf, b_glu.reshape(1, width),
      norm_g.reshape(1, width))
    return out.reshape(batch * seq, width)


def _layer_norm_rows(y, g, b):
    mu = jnp.mean(y, axis=-1, keepdims=True)
    yc = y - mu
    var = jnp.mean(yc * yc, axis=-1, keepdims=True)
    return yc * lax.rsqrt(var + LN_EPS) * g + b


def _mix_kernel(*refs, dilations):
    nb = len(dilations)
    o_refs, l_refs = refs[:nb], refs[nb:2 * nb]
    lru_ref, s5_ref, h_ref, w_ref, ga_ref, lg_ref, lb_ref, out_ref, tok_ref = refs[2 * nb:]
    npair = o_refs[0].shape[1]
    tm = h_ref.shape[0]

    part = tm // MIX_ROW_PARTS

    def token_major(ref, d, slot, p, r0):
        if d == 1:
            return ref[0, p, r0:r0 + part, :].astype(F32)
        for c in range(d):
            tok_ref[slot, pl.ds(r0 + c, part // d, stride=d), :] = (
                ref[c, p, r0 // d:(r0 + part) // d, :].astype(F32))
        return tok_ref[slot, r0:r0 + part, :]

    ga = ga_ref[...]
    for part_idx in range(MIX_ROW_PARTS):
        r0 = part_idx * part
        rows = slice(r0, r0 + part)
        attn = []
        ssq = jnp.zeros((part, 1), F32)
        for p in range(npair):
            outs = [token_major(o_refs[i], d, 2 * (nb * p + i), p, r0) for i, d in enumerate(dilations)]
            lses = [token_major(l_refs[i], d, 2 * (nb * p + i) + 1, p, r0) for i, d in enumerate(dilations)]
            m = functools.reduce(jnp.maximum, lses)
            es = [jnp.exp(l - m) for l in lses]
            a = sum(e * o for e, o in zip(es, outs)) / sum(es)
            ssq = ssq + jnp.sum(a * a, axis=-1, keepdims=True)
            attn.append(a)
        scale = lax.rsqrt(ssq * (1.0 / (npair * HEAD_PAIR)) + RMS_EPS)
        attn_n = [(a * scale * ga[:, p * HEAD_PAIR:(p + 1) * HEAD_PAIR]).astype(BF16) for p, a in enumerate(attn)]
        mixed = jnp.concatenate(attn_n + [lru_ref[rows, :], s5_ref[rows, :]], axis=1)
        mix = jnp.dot(mixed, w_ref[...], preferred_element_type=F32)
        y = DEEPNORM_ALPHA * h_ref[rows, :] + mix
        out_ref[rows, :] = _layer_norm_rows(y, lg_ref[...], lb_ref[...])


def _mix(attn_outs, dilations, lru, s5, h, seq, w_out_bf, g_attn, ln_g, ln_b, tm):
    t, d = h.shape
    aw = lru.shape[1]
    sw = s5.shape[1]
    nseq = seq // tm
    npair = aw // HEAD_PAIR

    def rows(w):
        return pl.BlockSpec((tm, w), lambda i: (i, 0))

    def full(shape):
        return pl.BlockSpec(shape, lambda i: (0,) * len(shape))

    def class_major(dil):
        return pl.BlockSpec((None, dil, npair, tm // dil, HEAD_PAIR),
                            lambda i: (i // nseq, 0, 0, i % nseq, 0))

    branch_specs = [class_major(dil) for dil in dilations]
    return pl.pallas_call(
        functools.partial(_mix_kernel, dilations=tuple(dilations)),
        grid=(t // tm,),
        in_specs=branch_specs + branch_specs + [rows(aw), rows(sw), rows(d), full((d, d)),
                                                full((1, aw)), full((1, d)), full((1, d))],
        out_specs=rows(d),
        out_shape=jax.ShapeDtypeStruct((t, d), F32),
        scratch_shapes=[pltpu.VMEM((2 * len(dilations) * npair, tm, HEAD_PAIR), F32)],
        compiler_params=_params("parallel"),
        name="mix_out_ln",
    )(*[o for o, _ in attn_outs], *[l for _, l in attn_outs], lru, s5, h, w_out_bf,
      g_attn.reshape(1, aw), ln_g.reshape(1, d), ln_b.reshape(1, d))


def _ffn_kernel(h_ref, hp_ref, wu_ref, cw_ref, cb_ref, wd_ref, lg_ref, lb_ref, out_ref,
                *, tiles_per_seq, ff_chunk):
    i = pl.program_id(0)
    keep_prev = jnp.where(i % tiles_per_seq == 0, 0.0, 1.0).astype(F32)
    h = h_ref[...]
    he = jnp.concatenate([hp_ref[...] * keep_prev, h], axis=0).astype(BF16)
    d_ff = wd_ref.shape[0]

    def conv_branch(col0):
        cols = slice(col0, col0 + ff_chunk)
        up = jnp.dot(he, wu_ref[:, cols], preferred_element_type=F32)
        cw = cw_ref[:, cols]
        y = cw[FFN_CONV - 1:FFN_CONV, :] * up
        for k in range(1, FFN_CONV):
            tap = FFN_CONV - 1 - k
            y = y + cw[tap:tap + 1, :] * _shift_rows(up, k)
        return y[SUBLANES:, :] + cb_ref[:, cols]

    acc = None
    for j in range(d_ff // ff_chunk):
        gate = conv_branch(j * ff_chunk)
        val = conv_branch(d_ff + j * ff_chunk)
        act = (_gelu(gate) * val).astype(BF16)
        part = jnp.dot(act, wd_ref[j * ff_chunk:(j + 1) * ff_chunk, :], preferred_element_type=F32)
        acc = part if acc is None else acc + part
    y = DEEPNORM_ALPHA * h + acc
    out_ref[...] = _layer_norm_rows(y, lg_ref[...], lb_ref[...])


def _ffn(h, seq, w_up_bf, conv_w, conv_b, w_down_bf, ln_g, ln_b, tm, cw):
    t, d = h.shape
    d_ff = w_down_bf.shape[0]
    per8 = tm // SUBLANES

    def resident(shape):
        return pl.BlockSpec(shape, lambda i: (0,) * len(shape), pipeline_mode=pl.Buffered(1))

    return pl.pallas_call(
        functools.partial(_ffn_kernel, tiles_per_seq=seq // tm, ff_chunk=cw),
        grid=(t // tm,),
        in_specs=[
            pl.BlockSpec((tm, d), lambda i: (i, 0)),
            pl.BlockSpec((SUBLANES, d), lambda i: (jnp.maximum(i * per8 - 1, 0), 0)),
            resident((d, 2 * d_ff)),
            resident((FFN_CONV, 2 * d_ff)),
            resident((1, 2 * d_ff)),
            resident((d_ff, d)),
            resident((1, d)),
            resident((1, d)),
        ],
        out_specs=pl.BlockSpec((tm, d), lambda i: (i, 0)),
        out_shape=jax.ShapeDtypeStruct((t, d), F32),
        compiler_params=pltpu.CompilerParams(dimension_semantics=("parallel",),
                                             vmem_limit_bytes=FFN_VMEM_LIMIT),
        name="conv_ffn_ln",
    )(h, h, w_up_bf, conv_w, conv_b.reshape(1, 2 * d_ff), w_down_bf,
      ln_g.reshape(1, d), ln_b.reshape(1, d))


def kernel(x, w_in, lru_conv_w, lru_conv_b, lru_wr, lru_br, lru_wi, lru_bi, lru_lambda, s5_a_re, s5_a_im, s5_b_re, s5_b_im, s5_c_re, s5_c_im, s5_d, s5_log_step, s5_w_glu, s5_b_glu, mix_norm_g, w_out, ln1_g, ln1_b, w_up, ffn_conv_w, ffn_conv_b, w_down, ln2_g, ln2_b):
    batch, seq, d_model = x.shape
    depth = w_in.shape[0]
    lru_width = lru_conv_w.shape[-1]
    s5_width = s5_w_glu.shape[-1]
    attn_width = (w_in.shape[-1] - 2 * lru_width - s5_width) // 3
    assert attn_width % HEAD_PAIR == 0 and lru_width == attn_width
    assert all(w // d == ATTN_BLOCK for w, d in DILATED_PAIRS)
    dilations = [d for _, d in DILATED_PAIRS]
    s5_chunk = 256
    lru_chunk = 256
    row_tile = 512
    proj_tile = 1024
    ffn_tile = 512
    ff_chunk = 1536
    s5_col_block = (2 * lru_width) // s5_width
    assert s5_col_block * s5_width == 2 * lru_width

    cos_t, sin_t = _rope_tables(seq)
    all_tables = jax.vmap(functools.partial(_s5_tables, chunk=s5_chunk))(
        s5_a_re, s5_a_im, s5_b_re, s5_b_im, s5_c_re, s5_c_im, s5_log_step)
    all_w_gate = jnp.concatenate([jax.vmap(_block_diag)(lru_wr), jax.vmap(_block_diag)(lru_wi)], axis=2).astype(BF16)
    all_b_gate = jnp.concatenate([lru_br, lru_bi], axis=1)
    h = x.reshape(batch * seq, d_model)
    for l in range(depth):
        qkvs, rest = _proj(h, w_in[l].astype(BF16), cos_t, sin_t, batch, attn_width, dilations, proj_tile)
        attn_outs = [_attn_branch(qkv_cm, d) for qkv_cm, d in zip(qkvs, dilations)]
        lru = _lru(rest, batch, seq, lru_width, lru_conv_w[l], lru_conv_b[l], all_w_gate[l], all_b_gate[l],
                   lru_lambda[l], mix_norm_g[l, attn_width:attn_width + lru_width], lru_chunk)
        tables = tuple(t[l] for t in all_tables)
        s5 = _s5(rest, batch, seq, s5_col_block, s5_width, tables, s5_d[l].reshape(-1),
                 s5_w_glu[l].astype(BF16), s5_b_glu[l], mix_norm_g[l, attn_width + lru_width:], s5_chunk)
        h1 = _mix(attn_outs, dilations, lru, s5, h, seq, w_out[l].astype(BF16), mix_norm_g[l, :attn_width],
                  ln1_g[l], ln1_b[l], row_tile)
        h = _ffn(h1, seq, w_up[l].astype(BF16), ffn_conv_w[l], ffn_conv_b[l], w_down[l].astype(BF16),
                 ln2_g[l], ln2_b[l], ffn_tile, ff_chunk)
    return h.reshape(batch, seq, d_model)
```

```python
import functools

import jax
import jax.numpy as jnp
from jax import lax
from jax.experimental import pallas as pl
from jax.experimental.pallas import tpu as pltpu

F32 = jnp.float32
BF16 = jnp.bfloat16

HEAD_DIM = 64
HEAD_PAIR = 2 * HEAD_DIM
DILATED_PAIRS = ((128, 1), (512, 4), (2048, 16))
ATTN_BLOCK = 128
ROPE_THETA = 10000.0
LRU_CONV = 4
LRU_C = 8.0
FFN_CONV = 3
DEPTH = 2
DEEPNORM_ALPHA = (2 * DEPTH) ** 0.25
LN_EPS = 1e-5
RMS_EPS = 1e-6
NEG_BIG = -1e30
LN2 = 0.6931471805599453
Q_SCALE = HEAD_DIM ** -0.5 / LN2

LANES = 128
SUBLANES = 8
VMEM_LIMIT = 48 * 1024 * 1024
FFN_VMEM_LIMIT = 56 * 1024 * 1024

PROJ_TILE = 1024
MIX_TILE = 1024
MIX_ROW_PARTS = 4
FFN_TILE = 512
FFN_CHUNK = 1536
LRU_CHUNK = 256
LRU_SEQS_PER_STEP = 4
S5_CHUNK = 256
S5_SEQS_PER_STEP = 4
ATTN_QUERIES_PER_STEP = 1024
ATTN_PV_LAG = 2


def _params(*sem):
    return pltpu.CompilerParams(dimension_semantics=sem, vmem_limit_bytes=VMEM_LIMIT)


def _gelu(x):
    return jax.nn.gelu(x)


def _sigmoid(x):
    return 0.5 * jnp.tanh(0.5 * x) + 0.5


def _shift_rows(x, s):
    return pltpu.roll(x, s, 0)


def _run_pitch(run):
    return run + SUBLANES


def _store_runs(scr_ref, slab, x, run):
    pitch = _run_pitch(run)
    for s in range(SUBLANES):
        scr_ref[slab, s * pitch:s * pitch + run, :] = x[s * run:(s + 1) * run, :]


def _load_runs(scr_ref, slab, run):
    pitch = _run_pitch(run)
    return jnp.concatenate([scr_ref[slab, s * pitch:s * pitch + run, :] for s in range(SUBLANES)], axis=0)


def _step_rows(r, run):
    return pl.ds(r, SUBLANES, stride=_run_pitch(run))


def _sublane_delay(v, s, fill):
    sub = lax.broadcasted_iota(jnp.int32, v.shape, 0)
    return jnp.where(sub >= s, pltpu.roll(v, s, 0), fill)


def _proj_kernel(h_ref, w_ref, cos_ref, sin_ref, *refs, attn_width, dilations):
    out_refs = refs[:len(dilations)]
    rest_ref, stage_ref = refs[len(dilations):]
    n_slab = 3 * attn_width // LANES
    n_rot = 2 * attn_width // LANES
    rows = h_ref.shape[0]
    acc = jnp.dot(h_ref[...].astype(BF16), w_ref[...], preferred_element_type=F32)
    cos = cos_ref[...]
    sin = sin_ref[...]
    lane = lax.broadcasted_iota(jnp.int32, cos.shape, 1)
    first_half = (lane & (HEAD_DIM // 2)) == 0
    for cb in range(n_slab):
        x = acc[:, cb * LANES:(cb + 1) * LANES]
        if cb < n_rot:
            partner = jnp.where(first_half, pltpu.roll(x, LANES - HEAD_DIM // 2, 1),
                                pltpu.roll(x, HEAD_DIM // 2, 1))
            x = x * cos + partner * sin
        if cb < n_rot // 2:
            x = x * Q_SCALE
        stage_ref[cb] = x
    rest_ref[...] = acc[:, 3 * attn_width:]
    for out_ref, d in zip(out_refs, dilations):
        for c in range(d):
            for cb in range(n_slab):
                if d == 1:
                    piece = stage_ref[cb]
                else:
                    piece = stage_ref[cb, pl.ds(c, rows // d, stride=d), :]
                out_ref[c, cb] = piece.astype(BF16)


def _rope_tables(seq):
    half = HEAD_DIM // 2
    pos = jnp.arange(seq, dtype=F32)
    inv = ROPE_THETA ** (-jnp.arange(half, dtype=F32) * 2.0 / HEAD_DIM)
    ang = pos[:, None] * inv[None, :]
    cos = jnp.cos(ang)
    sin = jnp.sin(ang)
    cos_t = jnp.concatenate([cos, cos, cos, cos], axis=1)
    sin_t = jnp.concatenate([-sin, sin, -sin, sin], axis=1)
    return cos_t, sin_t


def _proj(h, w_in_bf, cos_t, sin_t, batch, attn_width, dilations, tm):
    t, d = h.shape
    d_in = w_in_bf.shape[1]
    seq = cos_t.shape[0]
    nseq = seq // tm
    rest_w = d_in - 3 * attn_width
    n_slab = 3 * attn_width // LANES
    qkv_specs = [pl.BlockSpec((None, dil, n_slab, tm // dil, LANES),
                              lambda i: (i // nseq, 0, 0, i % nseq, 0)) for dil in dilations]
    qkv_shapes = [jax.ShapeDtypeStruct((batch, dil, n_slab, seq // dil, LANES), BF16) for dil in dilations]
    outs = pl.pallas_call(
        functools.partial(_proj_kernel, attn_width=attn_width, dilations=tuple(dilations)),
        grid=(t // tm,),
        in_specs=[
            pl.BlockSpec((tm, d), lambda i: (i, 0)),
            pl.BlockSpec((d, d_in), lambda i: (0, 0)),
            pl.BlockSpec((tm, LANES), lambda i: (i % nseq, 0)),
            pl.BlockSpec((tm, LANES), lambda i: (i % nseq, 0)),
        ],
        out_specs=qkv_specs + [pl.BlockSpec((tm, rest_w), lambda i: (i, 0))],
        out_shape=qkv_shapes + [jax.ShapeDtypeStruct((t, rest_w), F32)],
        scratch_shapes=[pltpu.VMEM((n_slab, tm, LANES), F32)],
        compiler_params=_params("parallel"),
        name="proj_rope",
    )(h, w_in_bf, cos_t, sin_t)
    return outs[:-1], outs[-1]


def _attn_kernel(q_ref, k_ref, v_ref, *rest, has_prev):
    if has_prev:
        kp_ref, vp_ref, o_ref, l_ref = rest
    else:
        o_ref, l_ref = rest
    ncls, qb, _ = q_ref.shape
    win = (ATTN_BLOCK, 2 * ATTN_BLOCK)
    row = lax.broadcasted_iota(jnp.int32, win, 0)
    col = lax.broadcasted_iota(jnp.int32, win, 1)
    band_bias = jnp.where((col >= row) & (col <= row + ATTN_BLOCK), 0.0, NEG_BIG).astype(F32)
    if has_prev:
        no_prev = jnp.where(pl.program_id(3) > 0, 0.0, NEG_BIG).astype(F32)
        first_bias = band_bias + jnp.where(col < ATTN_BLOCK, no_prev, 0.0)
    else:
        first_bias = jnp.where(col < ATTN_BLOCK, NEG_BIG, band_bias).astype(F32)
    head0_q = lax.broadcasted_iota(jnp.int32, (ATTN_BLOCK, HEAD_PAIR), 1) < HEAD_DIM
    head0_kv = lax.broadcasted_iota(jnp.int32, (2 * ATTN_BLOCK, HEAD_PAIR), 1) < HEAD_DIM
    contract_last = (((1,), (1,)), ((), ()))
    items = [(cls, j, hd) for cls in range(ncls) for j in range(qb // ATTN_BLOCK) for hd in range(2)]

    def window(ref, prev_ref, cls, j):
        rows = slice(j * ATTN_BLOCK, (j + 1) * ATTN_BLOCK)
        if j > 0:
            return ref[cls, (j - 1) * ATTN_BLOCK:(j + 1) * ATTN_BLOCK, :]
        if has_prev:
            return jnp.concatenate([prev_ref[cls], ref[cls, rows, :]], axis=0)
        return jnp.concatenate([ref[cls, rows, :]] * 2, axis=0)

    def scores(cls, j, hd):
        q = q_ref[cls, j * ATTN_BLOCK:(j + 1) * ATTN_BLOCK, :]
        sel_q = head0_q if hd == 0 else jnp.logical_not(head0_q)
        qh = jnp.where(sel_q, q, jnp.zeros_like(q))
        kw = window(k_ref, kp_ref if has_prev else None, cls, j)
        bias = band_bias if j > 0 else first_bias
        return lax.dot_general(qh, kw, contract_last, preferred_element_type=F32) + bias

    def softmax_numerators(s):
        m = jnp.max(s, axis=1, keepdims=True)
        return jnp.exp2(s - m).astype(BF16), m

    def weighted_values(p, cls, j, hd):
        vw = window(v_ref, vp_ref if has_prev else None, cls, j)
        sel_kv = head0_kv if hd == 0 else jnp.logical_not(head0_kv)
        v_aug = jnp.where(sel_kv, vw, jnp.ones_like(vw))
        return jnp.dot(p, v_aug, preferred_element_type=F32)

    def finish(cls, j, pvs, maxes):
        rows = slice(j * ATTN_BLOCK, (j + 1) * ATTN_BLOCK)
        num = jnp.where(head0_q, pvs[0], pvs[1])
        den = pltpu.roll(jnp.where(head0_q, pvs[1], pvs[0]), HEAD_DIM, 1)
        o_ref[cls, rows, :] = (num / den).astype(o_ref.dtype)
        l_ref[cls, rows, :] = (jnp.where(head0_q, maxes[0], maxes[1]) + jnp.log2(den)) * LN2

    s_vals, p_vals, m_vals, pv_vals = {}, {}, {}, {}
    for t in range(len(items) + ATTN_PV_LAG):
        if t < len(items):
            s_vals[t] = scores(*items[t])
        if 1 <= t <= len(items):
            p_vals[t - 1], m_vals[t - 1] = softmax_numerators(s_vals.pop(t - 1))
        if t >= ATTN_PV_LAG:
            i = t - ATTN_PV_LAG
            cls, j, hd = items[i]
            pv_vals[i] = weighted_values(p_vals.pop(i), cls, j, hd)
            if hd == 1:
                finish(cls, j, [pv_vals.pop(i - 1), pv_vals.pop(i)], [m_vals.pop(i - 1), m_vals.pop(i)])


def _attn_branch(qkv_cm, dilation):
    batch, _, n_slab, m, _ = qkv_cm.shape
    npair = n_slab // 3
    qb = min(ATTN_QUERIES_PER_STEP, m)
    ncls = min(ATTN_QUERIES_PER_STEP // qb, dilation)
    nsub = qb // ATTN_BLOCK
    nblk = m // qb
    has_prev = nblk > 1

    def own(off):
        return pl.BlockSpec((None, ncls, None, qb, HEAD_PAIR),
                            lambda b, c, p, n: (b, c, off * npair + p, n, 0))

    def prev(off):
        return pl.BlockSpec((None, ncls, None, ATTN_BLOCK, HEAD_PAIR),
                            lambda b, c, p, n: (b, c, off * npair + p, jnp.maximum(n * nsub - 1, 0), 0))

    out_spec = pl.BlockSpec((None, ncls, None, qb, HEAD_PAIR), lambda b, c, p, n: (b, c, p, n, 0))
    out_shapes = [jax.ShapeDtypeStruct((batch, dilation, npair, m, HEAD_PAIR), dt) for dt in (BF16, F32)]
    in_specs = [own(0), own(1), own(2)] + ([prev(1), prev(2)] if has_prev else [])
    return pl.pallas_call(
        functools.partial(_attn_kernel, has_prev=has_prev),
        grid=(batch, dilation // ncls, npair, nblk),
        in_specs=in_specs,
        out_specs=[out_spec, out_spec],
        out_shape=out_shapes,
        compiler_params=_params("parallel", "parallel", "parallel", "arbitrary"),
        name=f"dilated_attn_d{dilation}",
    )(*([qkv_cm] * len(in_specs)))


def _lru_kernel(x_ref, xp_ref, g_ref, cw_ref, cb_ref, w_ref, b_ref, lam_ref, ng_ref,
                o_ref, carry_ref, a_scr, u_scr, *, width):
    n = pl.program_id(1)

    @pl.when(n == 0)
    def _():
        carry_ref[...] = jnp.zeros_like(carry_ref)

    for b in range(x_ref.shape[0]):
        _lru_sequence(n, x_ref.at[b], xp_ref.at[b], g_ref.at[b], cw_ref, cb_ref, w_ref, b_ref, lam_ref, ng_ref,
                      o_ref.at[b], carry_ref.at[b], a_scr.at[b], u_scr.at[b], width)


def _lru_sequence(n, x_ref, xp_ref, g_ref, cw_ref, cb_ref, w_ref, b_ref, lam_ref, ng_ref,
                  o_ref, carry_ref, a_scr, u_scr, width):
    x = x_ref[...]
    rows = x.shape[0]
    xp = xp_ref[...] * jnp.where(n > 0, 1.0, 0.0).astype(F32)
    xe = jnp.concatenate([xp, x], axis=0)
    cw = cw_ref[...]
    xc = cw[LRU_CONV - 1:LRU_CONV, :] * x + cb_ref[...]
    for k in range(1, LRU_CONV):
        tap = LRU_CONV - 1 - k
        xc = xc + cw[tap:tap + 1, :] * _shift_rows(xe, k)[SUBLANES:, :]
    gates = jnp.dot(xc.astype(BF16), w_ref[...], preferred_element_type=F32) + b_ref[...]
    r = _sigmoid(gates[:, :width])
    i = _sigmoid(gates[:, width:])
    lam = lam_ref[...]
    softplus_neg = jnp.maximum(-lam, 0.0) + jnp.log1p(jnp.exp(-jnp.abs(lam)))
    log_a = (-LRU_C) * r * softplus_neg
    a = jnp.exp(log_a)
    t = jnp.tanh(log_a)
    u = jnp.sqrt(-2.0 * t / (1.0 - t)) * (i * xc)
    run = rows // SUBLANES
    pitch = _run_pitch(run)
    h_slabs = []
    for sl in range(width // LANES):
        cols = slice(sl * LANES, (sl + 1) * LANES)
        _store_runs(a_scr, sl, a[:, cols], run)
        _store_runs(u_scr, sl, u[:, cols], run)
        h_r = a_cum = None
        for r in range(run):
            step = _step_rows(r, run)
            a_r = a_scr[sl, step, :]
            u_r = u_scr[sl, step, :]
            if r == 0:
                h_r, a_cum = u_r, a_r
            else:
                h_r, a_cum = a_r * h_r + u_r, a_r * a_cum
                u_scr[sl, step, :] = h_r
                a_scr[sl, step, :] = a_cum
        p_inc, e_inc = a_cum, h_r
        for sh in (1, 2, 4):
            e_inc, p_inc = (p_inc * _sublane_delay(e_inc, sh, 0.0) + e_inc,
                            p_inc * _sublane_delay(p_inc, sh, 1.0))
        carry = carry_ref[:, cols]
        state_in = _sublane_delay(e_inc, 1, 0.0) + _sublane_delay(p_inc, 1, 1.0) * carry
        carry_ref[:, cols] = (e_inc + p_inc * carry)[SUBLANES - 1:SUBLANES, :]
        pieces = []
        for s in range(SUBLANES):
            blk = slice(s * pitch, s * pitch + run)
            pieces.append(u_scr[sl, blk, :] + a_scr[sl, blk, :] * state_in[s:s + 1, :])
        h_slabs.append(jnp.concatenate(pieces, axis=0))
    h = jnp.concatenate(h_slabs, axis=1)
    out = h * _gelu(g_ref[...])
    ms = jnp.mean(out * out, axis=-1, keepdims=True)
    o_ref[...] = (out * lax.rsqrt(ms + RMS_EPS) * ng_ref[...]).astype(BF16)


def _lru(rest, batch, seq, width, conv_w, conv_b, w_gate_bf, b_gate, lam, norm_g, chunk):
    rest_v = rest.reshape(batch, seq, rest.shape[-1])
    nchunk = seq // chunk
    per8 = chunk // SUBLANES

    def full(shape):
        return pl.BlockSpec(shape, lambda b, n: (0,) * len(shape))

    nb = min(LRU_SEQS_PER_STEP, batch)
    out = pl.pallas_call(
        functools.partial(_lru_kernel, width=width),
        grid=(batch // nb, nchunk),
        in_specs=[
            pl.BlockSpec((nb, chunk, width), lambda b, n: (b, n, 0)),
            pl.BlockSpec((nb, SUBLANES, width), lambda b, n: (b, jnp.maximum(n * per8 - 1, 0), 0)),
            pl.BlockSpec((nb, chunk, width), lambda b, n: (b, n, 1)),
            full((LRU_CONV, width)),
            full((1, width)),
            full((width, 2 * width)),
            full((1, 2 * width)),
            full((1, width)),
            full((1, width)),
        ],
        out_specs=pl.BlockSpec((nb, chunk, width), lambda b, n: (b, n, 0)),
        out_shape=jax.ShapeDtypeStruct((batch, seq, width), BF16),
        scratch_shapes=[pltpu.VMEM((nb, 1, width), F32)]
        + [pltpu.VMEM((nb, width // LANES, SUBLANES * _run_pitch(chunk // SUBLANES), LANES), F32)] * 2,
        compiler_params=_params("parallel", "arbitrary"),
        name="rg_lru",
    )(rest_v, rest_v, rest_v, conv_w, conv_b.reshape(1, width), w_gate_bf,
      b_gate.reshape(1, 2 * width), lam.reshape(1, width), norm_g.reshape(1, width))
    return out.reshape(batch * seq, width)


def _block_diag(w):
    h, i, j = w.shape
    eye = jnp.eye(h, dtype=w.dtype)
    return jnp.einsum('hij,hg->higj', w, eye).reshape(h * i, h * j)


def _cmul(ar, ai, br, bi):
    return ar * br - ai * bi, ar * bi + ai * br


def _s5_kernel(u_ref, bbd_ref, pwr_ref, pwi_ref, mur_ref, mui_ref, cbd_ref, d_ref, wg_ref, bg_ref, ng_ref,
               o_ref, cr_ref, ci_ref, perm_ref, bu_ref, xb_ref, *, nstate):
    n = pl.program_id(1)

    @pl.when(n == 0)
    def _():
        cr_ref[...] = jnp.zeros_like(cr_ref)
        ci_ref[...] = jnp.zeros_like(ci_ref)

    nb, rows, width = u_ref.shape
    run = rows // SUBLANES
    u_run = jnp.concatenate([_s5_to_runs(u_ref.at[b], perm_ref.at[b]) for b in range(nb)], axis=0)
    bu = jnp.dot(u_run.astype(BF16), bbd_ref[...], preferred_element_type=F32)
    bu_ref[...] = bu.reshape(nb, rows, bu.shape[-1])
    for b in range(nb):
        _s5_scan(pwr_ref, pwi_ref, mur_ref, mui_ref, bu_ref.at[b], xb_ref.at[b], cr_ref.at[b], ci_ref.at[b], nstate)
    xb = xb_ref[...].reshape(nb * rows, xb_ref.shape[-1])
    y = jnp.dot(xb, cbd_ref[...], preferred_element_type=F32) + d_ref[...] * u_run
    y = _gelu(y)
    z = jnp.dot(y.astype(BF16), wg_ref[...], preferred_element_type=F32) + bg_ref[...]
    out = y * _sigmoid(z)
    ms = jnp.mean(out * out, axis=-1, keepdims=True)
    out = out * lax.rsqrt(ms + RMS_EPS) * ng_ref[...]
    for b in range(nb):
        slabs = []
        for sl in range(width // LANES):
            for r in range(run):
                step = slice(b * rows + r * SUBLANES, b * rows + (r + 1) * SUBLANES)
                perm_ref[b, sl, _step_rows(r, run), :] = out[step, sl * LANES:(sl + 1) * LANES]
            slabs.append(_load_runs(perm_ref.at[b], sl, run))
        o_ref[b] = jnp.concatenate(slabs, axis=1).astype(BF16)


def _s5_to_runs(u_ref, perm_ref):
    u = u_ref[...]
    rows, width = u.shape
    run = rows // SUBLANES
    slabs = []
    for sl in range(width // LANES):
        _store_runs(perm_ref, sl, u[:, sl * LANES:(sl + 1) * LANES], run)
        slabs.append(jnp.concatenate([perm_ref[sl, _step_rows(r, run), :] for r in range(run)], axis=0))
    return jnp.concatenate(slabs, axis=1)


def _s5_scan(pwr_ref, pwi_ref, mur_ref, mui_ref, bu_ref, xb_ref, cr_ref, ci_ref, nstate):
    rows = bu_ref.shape[0]
    run = rows // SUBLANES
    for j in range(nstate // LANES):
        re_cols = slice(j * LANES, (j + 1) * LANES)
        im_cols = slice(nstate + j * LANES, nstate + (j + 1) * LANES)
        lr = pwr_ref[0:1, re_cols]
        li = pwi_ref[0:1, re_cols]
        xr = bu_ref[0:SUBLANES, re_cols]
        xi = bu_ref[0:SUBLANES, im_cols]
        for r in range(1, run):
            step = slice(r * SUBLANES, (r + 1) * SUBLANES)
            dr, di = _cmul(lr, li, xr, xi)
            xr = bu_ref[step, re_cols] + dr
            xi = bu_ref[step, im_cols] + di
            bu_ref[step, re_cols] = xr
            bu_ref[step, im_cols] = xi
        er, ei = xr, xi
        for sh in (1, 2, 4):
            k = run * sh - 1
            dr, di = _cmul(pwr_ref[k:k + 1, re_cols], pwi_ref[k:k + 1, re_cols],
                           _sublane_delay(er, sh, 0.0), _sublane_delay(ei, sh, 0.0))
            er, ei = er + dr, ei + di
        cr = cr_ref[:, re_cols]
        ci = ci_ref[:, re_cols]
        dr, di = _cmul(mur_ref[:, re_cols], mui_ref[:, re_cols], cr, ci)
        in_r = _sublane_delay(er, 1, 0.0) + dr
        in_i = _sublane_delay(ei, 1, 0.0) + di
        dr, di = _cmul(pwr_ref[rows - 1:rows, re_cols], pwi_ref[rows - 1:rows, re_cols], cr, ci)
        cr_ref[:, re_cols] = (er + dr)[SUBLANES - 1:SUBLANES, :]
        ci_ref[:, re_cols] = (ei + di)[SUBLANES - 1:SUBLANES, :]
        for r0 in range(0, run, 2):
            outs_r, outs_i = [], []
            for r in (r0, r0 + 1):
                step = slice(r * SUBLANES, (r + 1) * SUBLANES)
                dr, di = _cmul(pwr_ref[r:r + 1, re_cols], pwi_ref[r:r + 1, re_cols], in_r, in_i)
                outs_r.append(bu_ref[step, re_cols] + dr)
                outs_i.append(bu_ref[step, im_cols] + di)
            pair = slice(r0 * SUBLANES, (r0 + 2) * SUBLANES)
            xb_ref[pair, re_cols] = jnp.concatenate(outs_r, axis=0).astype(BF16)
            xb_ref[pair, im_cols] = jnp.concatenate(outs_i, axis=0).astype(BF16)


def _s5_tables(a_re, a_im, b_re, b_im, c_re, c_im, log_step, chunk):
    g, p = a_re.shape
    step = jnp.exp(log_step)[:, None]
    dt_re, dt_im = step * a_re, step * a_im
    mag = jnp.exp(dt_re)
    ab_re, ab_im = mag * jnp.cos(dt_im), mag * jnp.sin(dt_im)
    z_re, z_im = ab_re - 1.0, ab_im
    den = a_re * a_re + a_im * a_im
    f_re = (z_re * a_re + z_im * a_im) / den
    f_im = (z_im * a_re - z_re * a_im) / den
    bb_re = f_re[..., None] * b_re - f_im[..., None] * b_im
    bb_im = f_re[..., None] * b_im + f_im[..., None] * b_re
    eye = jnp.eye(g, dtype=F32)

    def in_map(bb):
        c = bb.shape[-1]
        return jnp.einsum('gpc,gh->gchp', bb, eye).reshape(g * c, g * p)

    def out_map(cc):
        c = cc.shape[1]
        return jnp.einsum('gcp,gh->gphc', cc, eye).reshape(g * p, g * c)

    bbd = jnp.concatenate([in_map(bb_re), in_map(bb_im)], axis=1)
    cbd = jnp.concatenate([out_map(c_re), out_map(-c_im)], axis=0)
    pr = ab_re.reshape(1, g * p)
    pi = ab_im.reshape(1, g * p)
    s = 1
    while s < chunk:
        lr, li = pr[s - 1:s], pi[s - 1:s]
        pr, pi = (jnp.concatenate([pr, pr * lr - pi * li], axis=0),
                  jnp.concatenate([pi, pr * li + pi * lr], axis=0))
        s *= 2
    run = chunk // SUBLANES
    run_rows = jnp.arange(1, SUBLANES) * run - 1
    mur = jnp.concatenate([jnp.ones((1, g * p), F32), pr[run_rows]], axis=0)
    mui = jnp.concatenate([jnp.zeros((1, g * p), F32), pi[run_rows]], axis=0)
    return bbd.astype(BF16), pr, pi, mur, mui, cbd.astype(BF16)


def _s5(rest, batch, seq, col_block, width, tables, d, w_glu_bf, b_glu, norm_g, chunk):
    bbd, pwr, pwi, mur, mui, cbd = tables
    nstate = pwr.shape[1]
    rest_v = rest.reshape(batch, seq, rest.shape[-1])

    def full(shape):
        return pl.BlockSpec(shape, lambda b, n: (0,) * len(shape))

    nb = min(S5_SEQS_PER_STEP, batch)
    out = pl.pallas_call(
        functools.partial(_s5_kernel, nstate=nstate),
        grid=(batch // nb, seq // chunk),
        in_specs=[
            pl.BlockSpec((nb, chunk, width), lambda b, n: (b, n, col_block)),
            full((width, 2 * nstate)),
            full((chunk, nstate)),
            full((chunk, nstate)),
            full((SUBLANES, nstate)),
            full((SUBLANES, nstate)),
            full((2 * nstate, width)),
            full((1, width)),
            full((width, width)),
            full((1, width)),
            full((1, width)),
        ],
        out_specs=pl.BlockSpec((nb, chunk, width), lambda b, n: (b, n, 0)),
        out_shape=jax.ShapeDtypeStruct((batch, seq, width), BF16),
        scratch_shapes=[pltpu.VMEM((nb, 1, nstate), F32)] * 2 + [
            pltpu.VMEM((nb, width // LANES, SUBLANES * _run_pitch(chunk // SUBLANES), LANES), F32),
            pltpu.VMEM((nb, chunk, 2 * nstate), F32),
            pltpu.VMEM((nb, chunk, 2 * nstate), BF16),
        ],
        compiler_params=_params("parallel", "arbitrary"),
        name="s5_scan",
    )(rest_v, bbd, pwr, pwi, mur, mui, cbd, d.reshape(1, width), w_glu_bf, b_glu.reshape(1, width),
      norm_g.reshape(1, width))
    return out.reshape(batch * seq, width)


def _layer_norm_rows(y, g, b):
    mu = jnp.mean(y, axis=-1, keepdims=True)
    yc = y - mu
    var = jnp.mean(yc * yc, axis=-1, keepdims=True)
    return yc * lax.rsqrt(var + LN_EPS) * g + b


def _mix_kernel(*refs, dilations):
    nb = len(dilations)
    o_refs, l_refs = refs[:nb], refs[nb:2 * nb]
    lru_ref, s5_ref, h_ref, w_ref, ga_ref, lg_ref, lb_ref, out_ref, tok_ref = refs[2 * nb:]
    npair = o_refs[0].shape[1]
    tm = h_ref.shape[0]

    part = tm // MIX_ROW_PARTS

    def token_major(ref, d, slot, p, r0):
        if d == 1:
            return ref[0, p, r0:r0 + part, :].astype(F32)
        for c in range(d):
            tok_ref[slot, pl.ds(r0 + c, part // d, stride=d), :] = (
                ref[c, p, r0 // d:(r0 + part) // d, :].astype(F32))
        return tok_ref[slot, r0:r0 + part, :]

    ga = ga_ref[...]
    for part_idx in range(MIX_ROW_PARTS):
        r0 = part_idx * part
        rows = slice(r0, r0 + part)
        attn = []
        ssq = jnp.zeros((part, 1), F32)
        for p in range(npair):
            outs = [token_major(o_refs[i], d, 2 * (nb * p + i), p, r0) for i, d in enumerate(dilations)]
            lses = [token_major(l_refs[i], d, 2 * (nb * p + i) + 1, p, r0) for i, d in enumerate(dilations)]
            m = functools.reduce(jnp.maximum, lses)
            es = [jnp.exp(l - m) for l in lses]
            a = sum(e * o for e, o in zip(es, outs)) / sum(es)
            ssq = ssq + jnp.sum(a * a, axis=-1, keepdims=True)
            attn.append(a)
        scale = lax.rsqrt(ssq * (1.0 / (npair * HEAD_PAIR)) + RMS_EPS)
        attn_n = [(a * scale * ga[:, p * HEAD_PAIR:(p + 1) * HEAD_PAIR]).astype(BF16) for p, a in enumerate(attn)]
        mixed = jnp.concatenate(attn_n + [lru_ref[rows, :], s5_ref[rows, :]], axis=1)
        mix = jnp.dot(mixed, w_ref[...], preferred_element_type=F32)
        y = DEEPNORM_ALPHA * h_ref[rows, :] + mix
        out_ref[rows, :] = _layer_norm_rows(y, lg_ref[...], lb_ref[...])


def _mix(attn_outs, dilations, lru, s5, h, seq, w_out_bf, g_attn, ln_g, ln_b, tm):
    t, d = h.shape
    aw = lru.shape[1]
    sw = s5.shape[1]
    nseq = seq // tm
    npair = aw // HEAD_PAIR

    def rows(w):
        return pl.BlockSpec((tm, w), lambda i: (i, 0))

    def full(shape):
        return pl.BlockSpec(shape, lambda i: (0,) * len(shape))

    def class_major(dil):
        return pl.BlockSpec((None, dil, npair, tm // dil, HEAD_PAIR),
                            lambda i: (i // nseq, 0, 0, i % nseq, 0))

    branch_specs = [class_major(dil) for dil in dilations]
    return pl.pallas_call(
        functools.partial(_mix_kernel, dilations=tuple(dilations)),
        grid=(t // tm,),
        in_specs=branch_specs + branch_specs + [rows(aw), rows(sw), rows(d), full((d, d)),
                                                full((1, aw)), full((1, d)), full((1, d))],
        out_specs=rows(d),
        out_shape=jax.ShapeDtypeStruct((t, d), F32),
        scratch_shapes=[pltpu.VMEM((2 * len(dilations) * npair, tm, HEAD_PAIR), F32)],
        compiler_params=_params("parallel"),
        name="mix_out_ln",
    )(*[o for o, _ in attn_outs], *[l for _, l in attn_outs], lru, s5, h, w_out_bf,
      g_attn.reshape(1, aw), ln_g.reshape(1, d), ln_b.reshape(1, d))


def _ffn_kernel(h_ref, hp_ref, wu_ref, cw_ref, cb_ref, wd_ref, lg_ref, lb_ref, out_ref,
                *, tiles_per_seq, ff_chunk):
    i = pl.program_id(0)
    keep_prev = jnp.where(i % tiles_per_seq == 0, 0.0, 1.0).astype(F32)
    h = h_ref[...]
    he = jnp.concatenate([hp_ref[...] * keep_prev, h], axis=0).astype(BF16)
    d_ff = wd_ref.shape[0]

    def conv_branch(col0):
        cols = slice(col0, col0 + ff_chunk)
        up = jnp.dot(he, wu_ref[:, cols], preferred_element_type=F32)
        cw = cw_ref[:, cols]
        y = cw[FFN_CONV - 1:FFN_CONV, :] * up
        for k in range(1, FFN_CONV):
            tap = FFN_CONV - 1 - k
            y = y + cw[tap:tap + 1, :] * _shift_rows(up, k)
        return y[SUBLANES:, :] + cb_ref[:, cols]

    acc = None
    for j in range(d_ff // ff_chunk):
        gate = conv_branch(j * ff_chunk)
        val = conv_branch(d_ff + j * ff_chunk)
        act = (_gelu(gate) * val).astype(BF16)
        part = jnp.dot(act, wd_ref[j * ff_chunk:(j + 1) * ff_chunk, :], preferred_element_type=F32)
        acc = part if acc is None else acc + part
    y = DEEPNORM_ALPHA * h + acc
    out_ref[...] = _layer_norm_rows(y, lg_ref[...], lb_ref[...])


def _ffn(h, seq, w_up_bf, conv_w, conv_b, w_down_bf, ln_g, ln_b, tm, cw):
    t, d = h.shape
    d_ff = w_down_bf.shape[0]
    per8 = tm // SUBLANES

    def resident(shape):
        return pl.BlockSpec(shape, lambda i: (0,) * len(shape), pipeline_mode=pl.Buffered(1))

    return pl.pallas_call(
        functools.partial(_ffn_kernel, tiles_per_seq=seq // tm, ff_chunk=cw),
        grid=(t // tm,),
        in_specs=[
            pl.BlockSpec((tm, d), lambda i: (i, 0)),
            pl.BlockSpec((SUBLANES, d), lambda i: (jnp.maximum(i * per8 - 1, 0), 0)),
            resident((d, 2 * d_ff)),
            resident((FFN_CONV, 2 * d_ff)),
            resident((1, 2 * d_ff)),
            resident((d_ff, d)),
            resident((1, d)),
            resident((1, d)),
        ],
        out_specs=pl.BlockSpec((tm, d), lambda i: (i, 0)),
        out_shape=jax.ShapeDtypeStruct((t, d), F32),
        compiler_params=pltpu.CompilerParams(dimension_semantics=("parallel",),
                                             vmem_limit_bytes=FFN_VMEM_LIMIT),
        name="conv_ffn_ln",
    )(h, h, w_up_bf, conv_w, conv_b.reshape(1, 2 * d_ff), w_down_bf,
      ln_g.reshape(1, d), ln_b.reshape(1, d))


def kernel(x, w_in, lru_conv_w, lru_conv_b, lru_wr, lru_br, lru_wi, lru_bi, lru_lambda, s5_a_re, s5_a_im, s5_b_re, s5_b_im, s5_c_re, s5_c_im, s5_d, s5_log_step, s5_w_glu, s5_b_glu, mix_norm_g, w_out, ln1_g, ln1_b, w_up, ffn_conv_w, ffn_conv_b, w_down, ln2_g, ln2_b):
    batch, seq, d_model = x.shape
    depth = w_in.shape[0]
    lru_width = lru_conv_w.shape[-1]
    s5_width = s5_w_glu.shape[-1]
    attn_width = (w_in.shape[-1] - 2 * lru_width - s5_width) // 3
    assert attn_width % HEAD_PAIR == 0 and lru_width == attn_width
    assert all(w // d == ATTN_BLOCK for w, d in DILATED_PAIRS)
    dilations = [d for _, d in DILATED_PAIRS]
    s5_col_block = (2 * lru_width) // s5_width
    assert s5_col_block * s5_width == 2 * lru_width

    cos_t, sin_t = _rope_tables(seq)
    all_tables = jax.vmap(functools.partial(_s5_tables, chunk=S5_CHUNK))(
        s5_a_re, s5_a_im, s5_b_re, s5_b_im, s5_c_re, s5_c_im, s5_log_step)
    all_w_gate = jnp.concatenate([jax.vmap(_block_diag)(lru_wr), jax.vmap(_block_diag)(lru_wi)], axis=2).astype(BF16)
    all_b_gate = jnp.concatenate([lru_br, lru_bi], axis=1)
    h = x.reshape(batch * seq, d_model)
    for l in range(depth):
        qkvs, rest = _proj(h, w_in[l].astype(BF16), cos_t, sin_t, batch, attn_width, dilations, PROJ_TILE)
        attn_outs = [_attn_branch(qkv_cm, d) for qkv_cm, d in zip(qkvs, dilations)]
        lru = _lru(rest, batch, seq, lru_width, lru_conv_w[l], lru_conv_b[l], all_w_gate[l], all_b_gate[l],
                   lru_lambda[l], mix_norm_g[l, attn_width:attn_width + lru_width], LRU_CHUNK)
        tables = tuple(t[l] for t in all_tables)
        s5 = _s5(rest, batch, seq, s5_col_block, s5_width, tables, s5_d[l].reshape(-1),
                 s5_w_glu[l].astype(BF16), s5_b_glu[l], mix_norm_g[l, attn_width + lru_width:], S5_CHUNK)
        h1 = _mix(attn_outs, dilations, lru, s5, h, seq, w_out[l].astype(BF16), mix_norm_g[l, :attn_width],
                  ln1_g[l], ln1_b[l], MIX_TILE)
        h = _ffn(h1, seq, w_up[l].astype(BF16), ffn_conv_w[l], ffn_conv_b[l], w_down[l].astype(BF16),
                 ln2_g[l], ln2_b[l], FFN_TILE, FFN_CHUNK)
    return h.reshape(batch, seq, d_model)
```

```python
import functools

import jax
import jax.numpy as jnp
from jax import lax
from jax.experimental import pallas as pl
from jax.experimental.pallas import tpu as pltpu

F32 = jnp.float32
BF16 = jnp.bfloat16

HEAD_DIM = 64
HEAD_PAIR = 2 * HEAD_DIM
DILATED_PAIRS = ((128, 1), (512, 4), (2048, 16))
ATTN_BLOCK = 128
ROPE_THETA = 10000.0
LRU_CONV = 4
LRU_C = 8.0
FFN_CONV = 3
DEPTH = 2
DEEPNORM_ALPHA = (2 * DEPTH) ** 0.25
LN_EPS = 1e-5
RMS_EPS = 1e-6
NEG_BIG = -1e30
LN2 = 0.6931471805599453
Q_SCALE = HEAD_DIM ** -0.5 / LN2

LANES = 128
SUBLANES = 8
VMEM_LIMIT = 48 * 1024 * 1024
FFN_VMEM_LIMIT = 56 * 1024 * 1024

PROJ_TILE = 1024
MIX_TILE = 1024
MIX_ROW_PARTS = 4
FFN_TILE = 512
FFN_CHUNK = 1536
LRU_CHUNK = 256
LRU_SEQS_PER_STEP = 4
S5_CHUNK = 256
S5_SEQS_PER_STEP = 4
ATTN_QUERIES_PER_STEP = 1024
ATTN_PV_LAG = 2


def _params(*sem):
    return pltpu.CompilerParams(dimension_semantics=sem, vmem_limit_bytes=VMEM_LIMIT)


def _gelu(x):
    return jax.nn.gelu(x)


def _sigmoid(x):
    return 0.5 * jnp.tanh(0.5 * x) + 0.5


def _shift_rows(x, s):
    return pltpu.roll(x, s, 0)


def _run_pitch(run):
    return run + SUBLANES


def _store_runs(scr_ref, slab, x, run):
    pitch = _run_pitch(run)
    for s in range(SUBLANES):
        scr_ref[slab, s * pitch:s * pitch + run, :] = x[s * run:(s + 1) * run, :]


def _load_runs(scr_ref, slab, run):
    pitch = _run_pitch(run)
    return jnp.concatenate([scr_ref[slab, s * pitch:s * pitch + run, :] for s in range(SUBLANES)], axis=0)


def _step_rows(r, run):
    return pl.ds(r, SUBLANES, stride=_run_pitch(run))


def _sublane_delay(v, s, fill):
    sub = lax.broadcasted_iota(jnp.int32, v.shape, 0)
    return jnp.where(sub >= s, pltpu.roll(v, s, 0), fill)


def _proj_kernel(h_ref, w_ref, cos_ref, sin_ref, *refs, attn_width, dilations):
    out_refs = refs[:len(dilations)]
    rest_ref, stage_ref = refs[len(dilations):]
    n_slab = 3 * attn_width // LANES
    n_rot = 2 * attn_width // LANES
    rows = h_ref.shape[0]
    acc = jnp.dot(h_ref[...].astype(BF16), w_ref[...], preferred_element_type=F32)
    cos = cos_ref[...]
    sin = sin_ref[...]
    lane = lax.broadcasted_iota(jnp.int32, cos.shape, 1)
    first_half = (lane & (HEAD_DIM // 2)) == 0
    for cb in range(n_slab):
        x = acc[:, cb * LANES:(cb + 1) * LANES]
        if cb < n_rot:
            partner = jnp.where(first_half, pltpu.roll(x, LANES - HEAD_DIM // 2, 1),
                                pltpu.roll(x, HEAD_DIM // 2, 1))
            x = x * cos + partner * sin
        if cb < n_rot // 2:
            x = x * Q_SCALE
        stage_ref[cb, 0:rows, :] = x
    rest_ref[...] = acc[:, 3 * attn_width:]
    for out_ref, d in zip(out_refs, dilations):
        for c in range(d):
            for cb in range(n_slab):
                if d == 1:
                    piece = stage_ref[cb, 0:rows, :]
                else:
                    piece = stage_ref[cb, pl.ds(c, rows // d, stride=d), :]
                out_ref[c, cb] = piece.astype(BF16)


def _rope_tables(seq):
    half = HEAD_DIM // 2
    pos = jnp.arange(seq, dtype=F32)
    inv = ROPE_THETA ** (-jnp.arange(half, dtype=F32) * 2.0 / HEAD_DIM)
    ang = pos[:, None] * inv[None, :]
    cos = jnp.cos(ang)
    sin = jnp.sin(ang)
    cos_t = jnp.concatenate([cos, cos, cos, cos], axis=1)
    sin_t = jnp.concatenate([-sin, sin, -sin, sin], axis=1)
    return cos_t, sin_t


def _proj(h, w_in_bf, cos_t, sin_t, batch, attn_width, dilations, tm):
    t, d = h.shape
    d_in = w_in_bf.shape[1]
    seq = cos_t.shape[0]
    nseq = seq // tm
    rest_w = d_in - 3 * attn_width
    n_slab = 3 * attn_width // LANES
    qkv_specs = [pl.BlockSpec((None, dil, n_slab, tm // dil, LANES),
                              lambda i: (i // nseq, 0, 0, i % nseq, 0)) for dil in dilations]
    qkv_shapes = [jax.ShapeDtypeStruct((batch, dil, n_slab, seq // dil, LANES), BF16) for dil in dilations]
    outs = pl.pallas_call(
        functools.partial(_proj_kernel, attn_width=attn_width, dilations=tuple(dilations)),
        grid=(t // tm,),
        in_specs=[
            pl.BlockSpec((tm, d), lambda i: (i, 0)),
            pl.BlockSpec((d, d_in), lambda i: (0, 0)),
            pl.BlockSpec((tm, LANES), lambda i: (i % nseq, 0)),
            pl.BlockSpec((tm, LANES), lambda i: (i % nseq, 0)),
        ],
        out_specs=qkv_specs + [pl.BlockSpec((tm, rest_w), lambda i: (i, 0))],
        out_shape=qkv_shapes + [jax.ShapeDtypeStruct((t, rest_w), F32)],
        scratch_shapes=[pltpu.VMEM((n_slab, tm + SUBLANES, LANES), F32)],
        compiler_params=_params("parallel"),
        name="proj_rope",
    )(h, w_in_bf, cos_t, sin_t)
    return outs[:-1], outs[-1]


def _attn_kernel(q_ref, k_ref, v_ref, *rest, has_prev):
    if has_prev:
        kp_ref, vp_ref, o_ref, l_ref = rest
    else:
        o_ref, l_ref = rest
    ncls, qb, _ = q_ref.shape
    win = (ATTN_BLOCK, 2 * ATTN_BLOCK)
    row = lax.broadcasted_iota(jnp.int32, win, 0)
    col = lax.broadcasted_iota(jnp.int32, win, 1)
    band_bias = jnp.where((col >= row) & (col <= row + ATTN_BLOCK), 0.0, NEG_BIG).astype(F32)
    if has_prev:
        no_prev = jnp.where(pl.program_id(3) > 0, 0.0, NEG_BIG).astype(F32)
        first_bias = band_bias + jnp.where(col < ATTN_BLOCK, no_prev, 0.0)
    else:
        first_bias = jnp.where(col < ATTN_BLOCK, NEG_BIG, band_bias).astype(F32)
    head0_q = lax.broadcasted_iota(jnp.int32, (ATTN_BLOCK, HEAD_PAIR), 1) < HEAD_DIM
    head0_kv = lax.broadcasted_iota(jnp.int32, (2 * ATTN_BLOCK, HEAD_PAIR), 1) < HEAD_DIM
    contract_last = (((1,), (1,)), ((), ()))
    items = [(cls, j, hd) for cls in range(ncls) for j in range(qb // ATTN_BLOCK) for hd in range(2)]

    def window(ref, prev_ref, cls, j):
        rows = slice(j * ATTN_BLOCK, (j + 1) * ATTN_BLOCK)
        if j > 0:
            return ref[cls, (j - 1) * ATTN_BLOCK:(j + 1) * ATTN_BLOCK, :]
        if has_prev:
            return jnp.concatenate([prev_ref[cls], ref[cls, rows, :]], axis=0)
        return jnp.concatenate([ref[cls, rows, :]] * 2, axis=0)

    def scores(cls, j, hd):
        q = q_ref[cls, j * ATTN_BLOCK:(j + 1) * ATTN_BLOCK, :]
        sel_q = head0_q if hd == 0 else jnp.logical_not(head0_q)
        qh = jnp.where(sel_q, q, jnp.zeros_like(q))
        kw = window(k_ref, kp_ref if has_prev else None, cls, j)
        bias = band_bias if j > 0 else first_bias
        return lax.dot_general(qh, kw, contract_last, preferred_element_type=F32) + bias

    def softmax_numerators(s):
        m = jnp.max(s, axis=1, keepdims=True)
        return jnp.exp2(s - m).astype(BF16), m

    def weighted_values(p, cls, j, hd):
        vw = window(v_ref, vp_ref if has_prev else None, cls, j)
        sel_kv = head0_kv if hd == 0 else jnp.logical_not(head0_kv)
        v_aug = jnp.where(sel_kv, vw, jnp.ones_like(vw))
        return jnp.dot(p, v_aug, preferred_element_type=F32)

    def finish(cls, j, pvs, maxes):
        rows = slice(j * ATTN_BLOCK, (j + 1) * ATTN_BLOCK)
        num = jnp.where(head0_q, pvs[0], pvs[1])
        den = pltpu.roll(jnp.where(head0_q, pvs[1], pvs[0]), HEAD_DIM, 1)
        o_ref[cls, rows, :] = (num / den).astype(o_ref.dtype)
        l_ref[cls, rows, :] = (jnp.where(head0_q, maxes[0], maxes[1]) + jnp.log2(den)) * LN2

    s_vals, p_vals, m_vals, pv_vals = {}, {}, {}, {}
    for t in range(len(items) + ATTN_PV_LAG):
        if t < len(items):
            s_vals[t] = scores(*items[t])
        if 1 <= t <= len(items):
            p_vals[t - 1], m_vals[t - 1] = softmax_numerators(s_vals.pop(t - 1))
        if t >= ATTN_PV_LAG:
            i = t - ATTN_PV_LAG
            cls, j, hd = items[i]
            pv_vals[i] = weighted_values(p_vals.pop(i), cls, j, hd)
            if hd == 1:
                finish(cls, j, [pv_vals.pop(i - 1), pv_vals.pop(i)], [m_vals.pop(i - 1), m_vals.pop(i)])


def _attn_branch(qkv_cm, dilation):
    batch, _, n_slab, m, _ = qkv_cm.shape
    npair = n_slab // 3
    qb = min(ATTN_QUERIES_PER_STEP, m)
    ncls = min(ATTN_QUERIES_PER_STEP // qb, dilation)
    nsub = qb // ATTN_BLOCK
    nblk = m // qb
    has_prev = nblk > 1

    def own(off):
        return pl.BlockSpec((None, ncls, None, qb, HEAD_PAIR),
                            lambda b, c, p, n: (b, c, off * npair + p, n, 0))

    def prev(off):
        return pl.BlockSpec((None, ncls, None, ATTN_BLOCK, HEAD_PAIR),
                            lambda b, c, p, n: (b, c, off * npair + p, jnp.maximum(n * nsub - 1, 0), 0))

    out_spec = pl.BlockSpec((None, ncls, None, qb, HEAD_PAIR), lambda b, c, p, n: (b, c, p, n, 0))
    out_shapes = [jax.ShapeDtypeStruct((batch, dilation, npair, m, HEAD_PAIR), dt) for dt in (BF16, F32)]
    in_specs = [own(0), own(1), own(2)] + ([prev(1), prev(2)] if has_prev else [])
    return pl.pallas_call(
        functools.partial(_attn_kernel, has_prev=has_prev),
        grid=(batch, dilation // ncls, npair, nblk),
        in_specs=in_specs,
        out_specs=[out_spec, out_spec],
        out_shape=out_shapes,
        compiler_params=_params("parallel", "parallel", "parallel", "arbitrary"),
        name=f"dilated_attn_d{dilation}",
    )(*([qkv_cm] * len(in_specs)))


def _lru_kernel(x_ref, xp_ref, g_ref, cw_ref, cb_ref, w_ref, b_ref, lam_ref, ng_ref,
                o_ref, carry_ref, au_scr, *, width):
    n = pl.program_id(1)

    @pl.when(n == 0)
    def _():
        carry_ref[...] = jnp.zeros_like(carry_ref)

    for b in range(x_ref.shape[0]):
        _lru_sequence(n, x_ref.at[b], xp_ref.at[b], g_ref.at[b], cw_ref, cb_ref, w_ref, b_ref, lam_ref, ng_ref,
                      o_ref.at[b], carry_ref.at[b], au_scr.at[b, 0], au_scr.at[b, 1], width)


def _lru_sequence(n, x_ref, xp_ref, g_ref, cw_ref, cb_ref, w_ref, b_ref, lam_ref, ng_ref,
                  o_ref, carry_ref, a_scr, u_scr, width):
    x = x_ref[...]
    rows = x.shape[0]
    xp = xp_ref[...] * jnp.where(n > 0, 1.0, 0.0).astype(F32)
    xe = jnp.concatenate([xp, x], axis=0)
    cw = cw_ref[...]
    xc = cw[LRU_CONV - 1:LRU_CONV, :] * x + cb_ref[...]
    for k in range(1, LRU_CONV):
        tap = LRU_CONV - 1 - k
        xc = xc + cw[tap:tap + 1, :] * _shift_rows(xe, k)[SUBLANES:, :]
    gates = jnp.dot(xc.astype(BF16), w_ref[...], preferred_element_type=F32) + b_ref[...]
    r = _sigmoid(gates[:, :width])
    i = _sigmoid(gates[:, width:])
    lam = lam_ref[...]
    softplus_neg = jnp.maximum(-lam, 0.0) + jnp.log1p(jnp.exp(-jnp.abs(lam)))
    log_a = (-LRU_C) * r * softplus_neg
    a = jnp.exp(log_a)
    t = jnp.tanh(log_a)
    u = jnp.sqrt(-2.0 * t / (1.0 - t)) * (i * xc)
    run = rows // SUBLANES
    pitch = _run_pitch(run)
    h_slabs = []
    for sl in range(width // LANES):
        cols = slice(sl * LANES, (sl + 1) * LANES)
        _store_runs(a_scr, sl, a[:, cols], run)
        _store_runs(u_scr, sl, u[:, cols], run)
        h_r = a_cum = None
        for r in range(run):
            step = _step_rows(r, run)
            a_r = a_scr[sl, step, :]
            u_r = u_scr[sl, step, :]
            if r == 0:
                h_r, a_cum = u_r, a_r
            else:
                h_r, a_cum = a_r * h_r + u_r, a_r * a_cum
                u_scr[sl, step, :] = h_r
                a_scr[sl, step, :] = a_cum
        p_inc, e_inc = a_cum, h_r
        for sh in (1, 2, 4):
            e_inc, p_inc = (p_inc * _sublane_delay(e_inc, sh, 0.0) + e_inc,
                            p_inc * _sublane_delay(p_inc, sh, 1.0))
        carry = carry_ref[:, cols]
        state_in = _sublane_delay(e_inc, 1, 0.0) + _sublane_delay(p_inc, 1, 1.0) * carry
        carry_ref[:, cols] = (e_inc + p_inc * carry)[SUBLANES - 1:SUBLANES, :]
        pieces = []
        for s in range(SUBLANES):
            blk = slice(s * pitch, s * pitch + run)
            pieces.append(u_scr[sl, blk, :] + a_scr[sl, blk, :] * state_in[s:s + 1, :])
        h_slabs.append(jnp.concatenate(pieces, axis=0))
    h = jnp.concatenate(h_slabs, axis=1)
    out = h * _gelu(g_ref[...])
    ms = jnp.mean(out * out, axis=-1, keepdims=True)
    o_ref[...] = (out * lax.rsqrt(ms + RMS_EPS) * ng_ref[...]).astype(BF16)


def _lru(rest, batch, seq, width, conv_w, conv_b, w_gate_bf, b_gate, lam, norm_g, chunk):
    rest_v = rest.reshape(batch, seq, rest.shape[-1])
    nchunk = seq // chunk
    per8 = chunk // SUBLANES

    def full(shape):
        return pl.BlockSpec(shape, lambda b, n: (0,) * len(shape))

    nb = min(LRU_SEQS_PER_STEP, batch)
    out = pl.pallas_call(
        functools.partial(_lru_kernel, width=width),
        grid=(batch // nb, nchunk),
        in_specs=[
            pl.BlockSpec((nb, chunk, width), lambda b, n: (b, n, 0)),
            pl.BlockSpec((nb, SUBLANES, width), lambda b, n: (b, jnp.maximum(n * per8 - 1, 0), 0)),
            pl.BlockSpec((nb, chunk, width), lambda b, n: (b, n, 1)),
            full((LRU_CONV, width)),
            full((1, width)),
            full((width, 2 * width)),
            full((1, 2 * width)),
            full((1, width)),
            full((1, width)),
        ],
        out_specs=pl.BlockSpec((nb, chunk, width), lambda b, n: (b, n, 0)),
        out_shape=jax.ShapeDtypeStruct((batch, seq, width), BF16),
        scratch_shapes=[
            pltpu.VMEM((nb, 1, width), F32),
            pltpu.VMEM((nb, 2, width // LANES, SUBLANES * _run_pitch(chunk // SUBLANES) + SUBLANES, LANES), F32),
        ],
        compiler_params=_params("parallel", "arbitrary"),
        name="rg_lru",
    )(rest_v, rest_v, rest_v, conv_w, conv_b.reshape(1, width), w_gate_bf,
      b_gate.reshape(1, 2 * width), lam.reshape(1, width), norm_g.reshape(1, width))
    return out.reshape(batch * seq, width)


def _block_diag(w):
    h, i, j = w.shape
    eye = jnp.eye(h, dtype=w.dtype)
    return jnp.einsum('hij,hg->higj', w, eye).reshape(h * i, h * j)


def _cmul(ar, ai, br, bi):
    return ar * br - ai * bi, ar * bi + ai * br


def _s5_kernel(u_ref, bbd_ref, pwr_ref, pwi_ref, mur_ref, mui_ref, cbd_ref, d_ref, wg_ref, bg_ref, ng_ref,
               o_ref, cr_ref, ci_ref, perm_ref, bu_ref, xb_ref, *, nstate):
    n = pl.program_id(1)

    @pl.when(n == 0)
    def _():
        cr_ref[...] = jnp.zeros_like(cr_ref)
        ci_ref[...] = jnp.zeros_like(ci_ref)

    nb, rows, width = u_ref.shape
    run = rows // SUBLANES
    u_run = jnp.concatenate([_s5_to_runs(u_ref.at[b], perm_ref.at[b]) for b in range(nb)], axis=0)
    bu = jnp.dot(u_run.astype(BF16), bbd_ref[...], preferred_element_type=F32)
    bu_ref[...] = bu.reshape(nb, rows, bu.shape[-1])
    for b in range(nb):
        _s5_scan(pwr_ref, pwi_ref, mur_ref, mui_ref, bu_ref.at[b], xb_ref.at[b], cr_ref.at[b], ci_ref.at[b], nstate)
    xb = xb_ref[...].reshape(nb * rows, xb_ref.shape[-1])
    y = jnp.dot(xb, cbd_ref[...], preferred_element_type=F32) + d_ref[...] * u_run
    y = _gelu(y)
    z = jnp.dot(y.astype(BF16), wg_ref[...], preferred_element_type=F32) + bg_ref[...]
    out = y * _sigmoid(z)
    ms = jnp.mean(out * out, axis=-1, keepdims=True)
    out = out * lax.rsqrt(ms + RMS_EPS) * ng_ref[...]
    for b in range(nb):
        slabs = []
        for sl in range(width // LANES):
            for r in range(run):
                step = slice(b * rows + r * SUBLANES, b * rows + (r + 1) * SUBLANES)
                perm_ref[b, sl, _step_rows(r, run), :] = out[step, sl * LANES:(sl + 1) * LANES]
            slabs.append(_load_runs(perm_ref.at[b], sl, run))
        o_ref[b] = jnp.concatenate(slabs, axis=1).astype(BF16)


def _s5_to_runs(u_ref, perm_ref):
    u = u_ref[...]
    rows, width = u.shape
    run = rows // SUBLANES
    slabs = []
    for sl in range(width // LANES):
        _store_runs(perm_ref, sl, u[:, sl * LANES:(sl + 1) * LANES], run)
        slabs.append(jnp.concatenate([perm_ref[sl, _step_rows(r, run), :] for r in range(run)], axis=0))
    return jnp.concatenate(slabs, axis=1)


def _s5_scan(pwr_ref, pwi_ref, mur_ref, mui_ref, bu_ref, xb_ref, cr_ref, ci_ref, nstate):
    rows = bu_ref.shape[0]
    run = rows // SUBLANES
    for j in range(nstate // LANES):
        cols = slice(j * LANES, (j + 1) * LANES)
        re_cols = slice(2 * j * LANES, (2 * j + 1) * LANES)
        im_cols = slice((2 * j + 1) * LANES, (2 * j + 2) * LANES)
        lr = pwr_ref[0:1, cols]
        li = pwi_ref[0:1, cols]
        xr = bu_ref[0:SUBLANES, re_cols]
        xi = bu_ref[0:SUBLANES, im_cols]
        for r in range(1, run):
            step = slice(r * SUBLANES, (r + 1) * SUBLANES)
            dr, di = _cmul(lr, li, xr, xi)
            xr = bu_ref[step, re_cols] + dr
            xi = bu_ref[step, im_cols] + di
            bu_ref[step, re_cols] = xr
            bu_ref[step, im_cols] = xi
        er, ei = xr, xi
        for sh in (1, 2, 4):
            k = run * sh - 1
            dr, di = _cmul(pwr_ref[k:k + 1, cols], pwi_ref[k:k + 1, cols],
                           _sublane_delay(er, sh, 0.0), _sublane_delay(ei, sh, 0.0))
            er, ei = er + dr, ei + di
        cr = cr_ref[:, cols]
        ci = ci_ref[:, cols]
        dr, di = _cmul(mur_ref[:, cols], mui_ref[:, cols], cr, ci)
        in_r = _sublane_delay(er, 1, 0.0) + dr
        in_i = _sublane_delay(ei, 1, 0.0) + di
        dr, di = _cmul(pwr_ref[rows - 1:rows, cols], pwi_ref[rows - 1:rows, cols], cr, ci)
        cr_ref[:, cols] = (er + dr)[SUBLANES - 1:SUBLANES, :]
        ci_ref[:, cols] = (ei + di)[SUBLANES - 1:SUBLANES, :]
        for r0 in range(0, run, 2):
            outs_r, outs_i = [], []
            for r in (r0, r0 + 1):
                step = slice(r * SUBLANES, (r + 1) * SUBLANES)
                dr, di = _cmul(pwr_ref[r:r + 1, cols], pwi_ref[r:r + 1, cols], in_r, in_i)
                outs_r.append(bu_ref[step, re_cols] + dr)
                outs_i.append(bu_ref[step, im_cols] + di)
            pair = slice(r0 * SUBLANES, (r0 + 2) * SUBLANES)
            xb_ref[pair, re_cols] = jnp.concatenate(outs_r, axis=0).astype(BF16)
            xb_ref[pair, im_cols] = jnp.concatenate(outs_i, axis=0).astype(BF16)


def _s5_tables(a_re, a_im, b_re, b_im, c_re, c_im, log_step, chunk):
    g, p = a_re.shape
    step = jnp.exp(log_step)[:, None]
    dt_re, dt_im = step * a_re, step * a_im
    mag = jnp.exp(dt_re)
    ab_re, ab_im = mag * jnp.cos(dt_im), mag * jnp.sin(dt_im)
    z_re, z_im = ab_re - 1.0, ab_im
    den = a_re * a_re + a_im * a_im
    f_re = (z_re * a_re + z_im * a_im) / den
    f_im = (z_im * a_re - z_re * a_im) / den
    bb_re = f_re[..., None] * b_re - f_im[..., None] * b_im
    bb_im = f_re[..., None] * b_im + f_im[..., None] * b_re
    eye = jnp.eye(g, dtype=F32)

    def in_map(bb):
        c = bb.shape[-1]
        return jnp.einsum('gpc,gh->gchp', bb, eye).reshape(g * c, g * p)

    def out_map(cc):
        c = cc.shape[1]
        return jnp.einsum('gcp,gh->gphc', cc, eye).reshape(g * p, g * c)

    n_strip = g * p // LANES
    bbd = jnp.stack([in_map(bb_re).reshape(-1, n_strip, LANES), in_map(bb_im).reshape(-1, n_strip, LANES)],
                    axis=2).reshape(-1, 2 * g * p)
    cbd = jnp.stack([out_map(c_re).reshape(n_strip, LANES, -1), out_map(-c_im).reshape(n_strip, LANES, -1)],
                    axis=1).reshape(2 * g * p, -1)
    pr = ab_re.reshape(1, g * p)
    pi = ab_im.reshape(1, g * p)
    s = 1
    while s < chunk:
        lr, li = pr[s - 1:s], pi[s - 1:s]
        pr, pi = (jnp.concatenate([pr, pr * lr - pi * li], axis=0),
                  jnp.concatenate([pi, pr * li + pi * lr], axis=0))
        s *= 2
    run = chunk // SUBLANES
    run_rows = jnp.arange(1, SUBLANES) * run - 1
    mur = jnp.concatenate([jnp.ones((1, g * p), F32), pr[run_rows]], axis=0)
    mui = jnp.concatenate([jnp.zeros((1, g * p), F32), pi[run_rows]], axis=0)
    return bbd.astype(BF16), pr, pi, mur, mui, cbd.astype(BF16)


def _s5(rest, batch, seq, col_block, width, tables, d, w_glu_bf, b_glu, norm_g, chunk):
    bbd, pwr, pwi, mur, mui, cbd = tables
    nstate = pwr.shape[1]
    rest_v = rest.reshape(batch, seq, rest.shape[-1])

    def full(shape):
        return pl.BlockSpec(shape, lambda b, n: (0,) * len(shape))

    nb = min(S5_SEQS_PER_STEP, batch)
    out = pl.pallas_call(
        functools.partial(_s5_kernel, nstate=nstate),
        grid=(batch // nb, seq // chunk),
        in_specs=[
            pl.BlockSpec((nb, chunk, width), lambda b, n: (b, n, col_block)),
            full((width, 2 * nstate)),
            full((chunk, nstate)),
            full((chunk, nstate)),
            full((SUBLANES, nstate)),
            full((SUBLANES, nstate)),
            full((2 * nstate, width)),
            full((1, width)),
            full((width, width)),
            full((1, width)),
            full((1, width)),
        ],
        out_specs=pl.BlockSpec((nb, chunk, width), lambda b, n: (b, n, 0)),
        out_shape=jax.ShapeDtypeStruct((batch, seq, width), BF16),
        scratch_shapes=[pltpu.VMEM((nb, 1, nstate), F32)] * 2 + [
            pltpu.VMEM((nb, width // LANES, SUBLANES * _run_pitch(chunk // SUBLANES), LANES), F32),
            pltpu.VMEM((nb, chunk, 2 * nstate), F32),
            pltpu.VMEM((nb, chunk, 2 * nstate), BF16),
        ],
        compiler_params=_params("parallel", "arbitrary"),
        name="s5_scan",
    )(rest_v, bbd, pwr, pwi, mur, mui, cbd, d.reshape(1, width), w_glu_bf, b_glu.reshape(1, width),
      norm_g.reshape(1, width))
    return out.reshape(batch * seq, width)


def _layer_norm_rows(y, g, b):
    mu = jnp.mean(y, axis=-1, keepdims=True)
    yc = y - mu
    var = jnp.mean(yc * yc, axis=-1, keepdims=True)
    return yc * lax.rsqrt(var + LN_EPS) * g + b


def _mix_kernel(*refs, dilations):
    nb = len(dilations)
    o_refs, l_refs = refs[:nb], refs[nb:2 * nb]
    lru_ref, s5_ref, h_ref, w_ref, ga_ref, lg_ref, lb_ref, out_ref, tok_ref = refs[2 * nb:]
    npair = o_refs[0].shape[1]
    tm = h_ref.shape[0]

    part = tm // MIX_ROW_PARTS

    def token_major(ref, d, slot, p, r0):
        if d == 1:
            return ref[0, p, r0:r0 + part, :].astype(F32)
        for c in range(d):
            tok_ref[slot, pl.ds(r0 + c, part // d, stride=d), :] = (
                ref[c, p, r0 // d:(r0 + part) // d, :].astype(F32))
        return tok_ref[slot, r0:r0 + part, :]

    ga = ga_ref[...]
    for part_idx in range(MIX_ROW_PARTS):
        r0 = part_idx * part
        rows = slice(r0, r0 + part)
        attn = []
        ssq = jnp.zeros((part, 1), F32)
        for p in range(npair):
            outs = [token_major(o_refs[i], d, 2 * (nb * p + i), p, r0) for i, d in enumerate(dilations)]
            lses = [token_major(l_refs[i], d, 2 * (nb * p + i) + 1, p, r0) for i, d in enumerate(dilations)]
            m = functools.reduce(jnp.maximum, lses)
            es = [jnp.exp(l - m) for l in lses]
            a = sum(e * o for e, o in zip(es, outs)) / sum(es)
            ssq = ssq + jnp.sum(a * a, axis=-1, keepdims=True)
            attn.append(a)
        scale = lax.rsqrt(ssq * (1.0 / (npair * HEAD_PAIR)) + RMS_EPS)
        attn_n = [(a * scale * ga[:, p * HEAD_PAIR:(p + 1) * HEAD_PAIR]).astype(BF16) for p, a in enumerate(attn)]
        mixed = jnp.concatenate(attn_n + [lru_ref[rows, :], s5_ref[rows, :]], axis=1)
        mix = jnp.dot(mixed, w_ref[...], preferred_element_type=F32)
        y = DEEPNORM_ALPHA * h_ref[rows, :] + mix
        out_ref[rows, :] = _layer_norm_rows(y, lg_ref[...], lb_ref[...])


def _mix(attn_outs, dilations, lru, s5, h, seq, w_out_bf, g_attn, ln_g, ln_b, tm):
    t, d = h.shape
    aw = lru.shape[1]
    sw = s5.shape[1]
    nseq = seq // tm
    npair = aw // HEAD_PAIR

    def rows(w):
        return pl.BlockSpec((tm, w), lambda i: (i, 0))

    def full(shape):
        return pl.BlockSpec(shape, lambda i: (0,) * len(shape))

    def class_major(dil):
        return pl.BlockSpec((None, dil, npair, tm // dil, HEAD_PAIR),
                            lambda i: (i // nseq, 0, 0, i % nseq, 0))

    branch_specs = [class_major(dil) for dil in dilations]
    return pl.pallas_call(
        functools.partial(_mix_kernel, dilations=tuple(dilations)),
        grid=(t // tm,),
        in_specs=branch_specs + branch_specs + [rows(aw), rows(sw), rows(d), full((d, d)),
                                                full((1, aw)), full((1, d)), full((1, d))],
        out_specs=rows(d),
        out_shape=jax.ShapeDtypeStruct((t, d), F32),
        scratch_shapes=[pltpu.VMEM((2 * len(dilations) * npair, tm + SUBLANES, HEAD_PAIR), F32)],
        compiler_params=_params("parallel"),
        name="mix_out_ln",
    )(*[o for o, _ in attn_outs], *[l for _, l in attn_outs], lru, s5, h, w_out_bf,
      g_attn.reshape(1, aw), ln_g.reshape(1, d), ln_b.reshape(1, d))


def _ffn_kernel(h_ref, hp_ref, wu_ref, cw_ref, cb_ref, wd_ref, lg_ref, lb_ref, out_ref,
                *, tiles_per_seq, ff_chunk):
    i = pl.program_id(0)
    keep_prev = jnp.where(i % tiles_per_seq == 0, 0.0, 1.0).astype(F32)
    h = h_ref[...]
    he = jnp.concatenate([hp_ref[...] * keep_prev, h], axis=0).astype(BF16)
    d_ff = wd_ref.shape[0]

    def conv_branch(col0):
        cols = slice(col0, col0 + ff_chunk)
        up = jnp.dot(he, wu_ref[:, cols], preferred_element_type=F32)
        cw = cw_ref[:, cols]
        y = cw[FFN_CONV - 1:FFN_CONV, :] * up
        for k in range(1, FFN_CONV):
            tap = FFN_CONV - 1 - k
            y = y + cw[tap:tap + 1, :] * _shift_rows(up, k)
        return y[SUBLANES:, :] + cb_ref[:, cols]

    acc = None
    for j in range(d_ff // ff_chunk):
        gate = conv_branch(j * ff_chunk)
        val = conv_branch(d_ff + j * ff_chunk)
        act = (_gelu(gate) * val).astype(BF16)
        part = jnp.dot(act, wd_ref[j * ff_chunk:(j + 1) * ff_chunk, :], preferred_element_type=F32)
        acc = part if acc is None else acc + part
    y = DEEPNORM_ALPHA * h + acc
    out_ref[...] = _layer_norm_rows(y, lg_ref[...], lb_ref[...])


def _ffn(h, seq, w_up_bf, conv_w, conv_b, w_down_bf, ln_g, ln_b, tm, cw):
    t, d = h.shape
    d_ff = w_down_bf.shape[0]
    per8 = tm // SUBLANES

    def resident(shape):
        return pl.BlockSpec(shape, lambda i: (0,) * len(shape), pipeline_mode=pl.Buffered(1))

    return pl.pallas_call(
        functools.partial(_ffn_kernel, tiles_per_seq=seq // tm, ff_chunk=cw),
        grid=(t // tm,),
        in_specs=[
            pl.BlockSpec((tm, d), lambda i: (i, 0)),
            pl.BlockSpec((SUBLANES, d), lambda i: (jnp.maximum(i * per8 - 1, 0), 0)),
            resident((d, 2 * d_ff)),
            resident((FFN_CONV, 2 * d_ff)),
            resident((1, 2 * d_ff)),
            resident((d_ff, d)),
            resident((1, d)),
            resident((1, d)),
        ],
        out_specs=pl.BlockSpec((tm, d), lambda i: (i, 0)),
        out_shape=jax.ShapeDtypeStruct((t, d), F32),
        compiler_params=pltpu.CompilerParams(dimension_semantics=("parallel",),
                                             vmem_limit_bytes=FFN_VMEM_LIMIT),
        name="conv_ffn_ln",
    )(h, h, w_up_bf, conv_w, conv_b.reshape(1, 2 * d_ff), w_down_bf,
      ln_g.reshape(1, d), ln_b.reshape(1, d))


def kernel(x, w_in, lru_conv_w, lru_conv_b, lru_wr, lru_br, lru_wi, lru_bi, lru_lambda, s5_a_re, s5_a_im, s5_b_re, s5_b_im, s5_c_re, s5_c_im, s5_d, s5_log_step, s5_w_glu, s5_b_glu, mix_norm_g, w_out, ln1_g, ln1_b, w_up, ffn_conv_w, ffn_conv_b, w_down, ln2_g, ln2_b):
    batch, seq, d_model = x.shape
    depth = w_in.shape[0]
    lru_width = lru_conv_w.shape[-1]
    s5_width = s5_w_glu.shape[-1]
    attn_width = (w_in.shape[-1] - 2 * lru_width - s5_width) // 3
    assert attn_width % HEAD_PAIR == 0 and lru_width == attn_width
    assert all(w // d == ATTN_BLOCK for w, d in DILATED_PAIRS)
    dilations = [d for _, d in DILATED_PAIRS]
    s5_col_block = (2 * lru_width) // s5_width
    assert s5_col_block * s5_width == 2 * lru_width

    cos_t, sin_t = _rope_tables(seq)
    all_tables = jax.vmap(functools.partial(_s5_tables, chunk=S5_CHUNK))(
        s5_a_re, s5_a_im, s5_b_re, s5_b_im, s5_c_re, s5_c_im, s5_log_step)
    all_w_gate = jnp.concatenate([jax.vmap(_block_diag)(lru_wr), jax.vmap(_block_diag)(lru_wi)], axis=2).astype(BF16)
    all_b_gate = jnp.concatenate([lru_br, lru_bi], axis=1)
    h = x.reshape(batch * seq, d_model)
    for l in range(depth):
        qkvs, rest = _proj(h, w_in[l].astype(BF16), cos_t, sin_t, batch, attn_width, dilations, PROJ_TILE)
        attn_outs = [_attn_branch(qkv_cm, d) for qkv_cm, d in zip(qkvs, dilations)]
        lru = _lru(rest, batch, seq, lru_width, lru_conv_w[l], lru_conv_b[l], all_w_gate[l], all_b_gate[l],
                   lru_lambda[l], mix_norm_g[l, attn_width:attn_width + lru_width], LRU_CHUNK)
        tables = tuple(t[l] for t in all_tables)
        s5 = _s5(rest, batch, seq, s5_col_block, s5_width, tables, s5_d[l].reshape(-1),
                 s5_w_glu[l].astype(BF16), s5_b_glu[l], mix_norm_g[l, attn_width + lru_width:], S5_CHUNK)
        h1 = _mix(attn_outs, dilations, lru, s5, h, seq, w_out[l].astype(BF16), mix_norm_g[l, :attn_width],
                  ln1_g[l], ln1_b[l], MIX_TILE)
        h = _ffn(h1, seq, w_up[l].astype(BF16), ffn_conv_w[l], ffn_conv_b[l], w_down[l].astype(BF16),
                 ln2_g[l], ln2_b[l], FFN_TILE, FFN_CHUNK)
    return h.reshape(batch, seq, d_model)
```

```python
import functools

import jax
import jax.numpy as jnp
from jax import lax
from jax.experimental import pallas as pl
from jax.experimental.pallas import tpu as pltpu

F32 = jnp.float32
BF16 = jnp.bfloat16

HEAD_DIM = 64
HEAD_PAIR = 2 * HEAD_DIM
DILATED_PAIRS = ((128, 1), (512, 4), (2048, 16))
ATTN_BLOCK = 128
ROPE_THETA = 10000.0
LRU_CONV = 4
LRU_C = 8.0
FFN_CONV = 3
DEPTH = 2
DEEPNORM_ALPHA = (2 * DEPTH) ** 0.25
LN_EPS = 1e-5
RMS_EPS = 1e-6
NEG_BIG = -1e30
LN2 = 0.6931471805599453
Q_SCALE = HEAD_DIM ** -0.5 / LN2

LANES = 128
SUBLANES = 8
VMEM_LIMIT = 48 * 1024 * 1024
FFN_VMEM_LIMIT = 56 * 1024 * 1024

PROJ_TILE = 1024
MIX_TILE = 1024
MIX_ROW_PARTS = 4
FFN_TILE = 512
FFN_CHUNK = 1536
LRU_CHUNK = 256
LRU_SEQS_PER_STEP = 4
S5_CHUNK = 256
S5_SEQS_PER_STEP = 4
ATTN_QUERIES_PER_STEP = 1024
ATTN_PV_LAG = 2


def _params(*sem):
    return pltpu.CompilerParams(dimension_semantics=sem, vmem_limit_bytes=VMEM_LIMIT)


def _gelu(x):
    return jax.nn.gelu(x)


def _sigmoid(x):
    return 0.5 * jnp.tanh(0.5 * x) + 0.5


def _shift_rows(x, s):
    return pltpu.roll(x, s, 0)


def _run_pitch(run):
    return run + SUBLANES


def _store_runs(scr_ref, slab, x, run):
    pitch = _run_pitch(run)
    for s in range(SUBLANES):
        scr_ref[slab, s * pitch:s * pitch + run, :] = x[s * run:(s + 1) * run, :]


def _load_runs(scr_ref, slab, run):
    pitch = _run_pitch(run)
    return jnp.concatenate([scr_ref[slab, s * pitch:s * pitch + run, :] for s in range(SUBLANES)], axis=0)


def _step_rows(r, run):
    return pl.ds(r, SUBLANES, stride=_run_pitch(run))


def _sublane_delay(v, s, fill):
    sub = lax.broadcasted_iota(jnp.int32, v.shape, 0)
    return jnp.where(sub >= s, pltpu.roll(v, s, 0), fill)


def _proj_kernel(h_ref, w_ref, cos_ref, sin_ref, *refs, attn_width, dilations):
    out_refs = refs[:len(dilations)]
    rest_ref, stage_ref = refs[len(dilations):]
    n_slab = 3 * attn_width // LANES
    n_rot = 2 * attn_width // LANES
    npair = attn_width // HEAD_PAIR
    rows = h_ref.shape[0]
    acc = jnp.dot(h_ref[...].astype(BF16), w_ref[...], preferred_element_type=F32)
    cos = cos_ref[...]
    sin = sin_ref[...]
    lane = lax.broadcasted_iota(jnp.int32, cos.shape, 1)
    first_half = (lane & (HEAD_DIM // 2)) == 0
    for cb in range(n_slab):
        x = acc[:, cb * LANES:(cb + 1) * LANES]
        if cb < n_rot:
            partner = jnp.where(first_half, pltpu.roll(x, LANES - HEAD_DIM // 2, 1),
                                pltpu.roll(x, HEAD_DIM // 2, 1))
            x = x * cos + partner * sin
        if cb < n_rot // 2:
            x = x * Q_SCALE
        stage_ref[cb, 0:rows, :] = x
    rest_ref[...] = acc[:, 3 * attn_width:]
    for out_ref, d in zip(out_refs, dilations):
        for c in range(d):
            for cb in range(n_slab):
                if d == 1:
                    piece = stage_ref[cb, 0:rows, :]
                else:
                    piece = stage_ref[cb, pl.ds(c, rows // d, stride=d), :]
                out_ref[c, 3 * (cb % npair) + cb // npair] = piece.astype(BF16)


def _rope_tables(seq):
    half = HEAD_DIM // 2
    pos = jnp.arange(seq, dtype=F32)
    inv = ROPE_THETA ** (-jnp.arange(half, dtype=F32) * 2.0 / HEAD_DIM)
    ang = pos[:, None] * inv[None, :]
    cos = jnp.cos(ang)
    sin = jnp.sin(ang)
    cos_t = jnp.concatenate([cos, cos, cos, cos], axis=1)
    sin_t = jnp.concatenate([-sin, sin, -sin, sin], axis=1)
    return cos_t, sin_t


def _proj(h, w_in_bf, cos_t, sin_t, batch, attn_width, dilations, tm):
    t, d = h.shape
    d_in = w_in_bf.shape[1]
    seq = cos_t.shape[0]
    nseq = seq // tm
    rest_w = d_in - 3 * attn_width
    n_slab = 3 * attn_width // LANES
    qkv_specs = [pl.BlockSpec((None, dil, n_slab, tm // dil, LANES),
                              lambda i: (i // nseq, 0, 0, i % nseq, 0)) for dil in dilations]
    qkv_shapes = [jax.ShapeDtypeStruct((batch, dil, n_slab, seq // dil, LANES), BF16) for dil in dilations]
    outs = pl.pallas_call(
        functools.partial(_proj_kernel, attn_width=attn_width, dilations=tuple(dilations)),
        grid=(t // tm,),
        in_specs=[
            pl.BlockSpec((tm, d), lambda i: (i, 0)),
            pl.BlockSpec((d, d_in), lambda i: (0, 0)),
            pl.BlockSpec((tm, LANES), lambda i: (i % nseq, 0)),
            pl.BlockSpec((tm, LANES), lambda i: (i % nseq, 0)),
        ],
        out_specs=qkv_specs + [pl.BlockSpec((tm, rest_w), lambda i: (i, 0))],
        out_shape=qkv_shapes + [jax.ShapeDtypeStruct((t, rest_w), F32)],
        scratch_shapes=[pltpu.VMEM((n_slab, tm + SUBLANES, LANES), F32)],
        compiler_params=_params("parallel"),
        name="proj_rope",
    )(h, w_in_bf, cos_t, sin_t)
    return outs[:-1], outs[-1]


def _attn_kernel(qkv_ref, *rest, has_prev):
    if has_prev:
        prev_ref, o_ref, l_ref = rest
    else:
        o_ref, l_ref = rest
    ncls, _, qb, _ = qkv_ref.shape
    q_slab, k_slab, v_slab = range(3)
    win = (ATTN_BLOCK, 2 * ATTN_BLOCK)
    row = lax.broadcasted_iota(jnp.int32, win, 0)
    col = lax.broadcasted_iota(jnp.int32, win, 1)
    band_bias = jnp.where((col >= row) & (col <= row + ATTN_BLOCK), 0.0, NEG_BIG).astype(F32)
    if has_prev:
        no_prev = jnp.where(pl.program_id(3) > 0, 0.0, NEG_BIG).astype(F32)
        first_bias = band_bias + jnp.where(col < ATTN_BLOCK, no_prev, 0.0)
    else:
        first_bias = jnp.where(col < ATTN_BLOCK, NEG_BIG, band_bias).astype(F32)
    head0_q = lax.broadcasted_iota(jnp.int32, (ATTN_BLOCK, HEAD_PAIR), 1) < HEAD_DIM
    head0_kv = lax.broadcasted_iota(jnp.int32, (2 * ATTN_BLOCK, HEAD_PAIR), 1) < HEAD_DIM
    contract_last = (((1,), (1,)), ((), ()))
    items = [(cls, j, hd) for cls in range(ncls) for j in range(qb // ATTN_BLOCK) for hd in range(2)]

    def window(slab, cls, j):
        rows = slice(j * ATTN_BLOCK, (j + 1) * ATTN_BLOCK)
        if j > 0:
            return qkv_ref[cls, slab, (j - 1) * ATTN_BLOCK:(j + 1) * ATTN_BLOCK, :]
        if has_prev:
            return jnp.concatenate([prev_ref[cls, slab], qkv_ref[cls, slab, rows, :]], axis=0)
        return jnp.concatenate([qkv_ref[cls, slab, rows, :]] * 2, axis=0)

    def scores(cls, j, hd):
        q = qkv_ref[cls, q_slab, j * ATTN_BLOCK:(j + 1) * ATTN_BLOCK, :]
        sel_q = head0_q if hd == 0 else jnp.logical_not(head0_q)
        qh = jnp.where(sel_q, q, jnp.zeros_like(q))
        kw = window(k_slab, cls, j)
        bias = band_bias if j > 0 else first_bias
        return lax.dot_general(qh, kw, contract_last, preferred_element_type=F32) + bias

    def softmax_numerators(s):
        m = jnp.max(s, axis=1, keepdims=True)
        return jnp.exp2(s - m).astype(BF16), m

    def weighted_values(p, cls, j, hd):
        vw = window(v_slab, cls, j)
        sel_kv = head0_kv if hd == 0 else jnp.logical_not(head0_kv)
        v_aug = jnp.where(sel_kv, vw, jnp.ones_like(vw))
        return jnp.dot(p, v_aug, preferred_element_type=F32)

    def finish(cls, j, pvs, maxes):
        rows = slice(j * ATTN_BLOCK, (j + 1) * ATTN_BLOCK)
        num = jnp.where(head0_q, pvs[0], pvs[1])
        den = pltpu.roll(jnp.where(head0_q, pvs[1], pvs[0]), HEAD_DIM, 1)
        o_ref[cls, rows, :] = (num / den).astype(o_ref.dtype)
        l_ref[cls, rows, :] = (jnp.where(head0_q, maxes[0], maxes[1]) + jnp.log2(den)) * LN2

    s_vals, p_vals, m_vals, pv_vals = {}, {}, {}, {}
    for t in range(len(items) + ATTN_PV_LAG):
        if t < len(items):
            s_vals[t] = scores(*items[t])
        if 1 <= t <= len(items):
            p_vals[t - 1], m_vals[t - 1] = softmax_numerators(s_vals.pop(t - 1))
        if t >= ATTN_PV_LAG:
            i = t - ATTN_PV_LAG
            cls, j, hd = items[i]
            pv_vals[i] = weighted_values(p_vals.pop(i), cls, j, hd)
            if hd == 1:
                finish(cls, j, [pv_vals.pop(i - 1), pv_vals.pop(i)], [m_vals.pop(i - 1), m_vals.pop(i)])


def _attn_branch(qkv_cm, dilation):
    batch, _, n_slab, m, _ = qkv_cm.shape
    npair = n_slab // 3
    qb = min(ATTN_QUERIES_PER_STEP, m)
    ncls = min(ATTN_QUERIES_PER_STEP // qb, dilation)
    nsub = qb // ATTN_BLOCK
    nblk = m // qb
    has_prev = nblk > 1

    own = pl.BlockSpec((None, ncls, 3, qb, HEAD_PAIR), lambda b, c, p, n: (b, c, p, n, 0))
    prev = pl.BlockSpec((None, ncls, 3, ATTN_BLOCK, HEAD_PAIR),
                        lambda b, c, p, n: (b, c, p, jnp.maximum(n * nsub - 1, 0), 0))
    out_spec = pl.BlockSpec((None, ncls, None, qb, HEAD_PAIR), lambda b, c, p, n: (b, c, p, n, 0))
    out_shapes = [jax.ShapeDtypeStruct((batch, dilation, npair, m, HEAD_PAIR), dt) for dt in (BF16, F32)]
    in_specs = [own] + ([prev] if has_prev else [])
    return pl.pallas_call(
        functools.partial(_attn_kernel, has_prev=has_prev),
        grid=(batch, dilation // ncls, npair, nblk),
        in_specs=in_specs,
        out_specs=[out_spec, out_spec],
        out_shape=out_shapes,
        compiler_params=_params("parallel", "parallel", "parallel", "arbitrary"),
        name=f"dilated_attn_d{dilation}",
    )(*([qkv_cm] * len(in_specs)))


def _lru_kernel(x_ref, xp_ref, g_ref, cw_ref, cb_ref, w_ref, b_ref, lam_ref, ng_ref,
                o_ref, carry_ref, au_scr, *, width):
    n = pl.program_id(1)

    @pl.when(n == 0)
    def _():
        carry_ref[...] = jnp.zeros_like(carry_ref)

    for b in range(x_ref.shape[0]):
        _lru_sequence(n, x_ref.at[b], xp_ref.at[b], g_ref.at[b], cw_ref, cb_ref, w_ref, b_ref, lam_ref, ng_ref,
                      o_ref.at[b], carry_ref.at[b], au_scr.at[b, 0], au_scr.at[b, 1], width)


def _lru_sequence(n, x_ref, xp_ref, g_ref, cw_ref, cb_ref, w_ref, b_ref, lam_ref, ng_ref,
                  o_ref, carry_ref, a_scr, u_scr, width):
    x = x_ref[...]
    rows = x.shape[0]
    xp = xp_ref[...] * jnp.where(n > 0, 1.0, 0.0).astype(F32)
    xe = jnp.concatenate([xp, x], axis=0)
    cw = cw_ref[...]
    xc = cw[LRU_CONV - 1:LRU_CONV, :] * x + cb_ref[...]
    for k in range(1, LRU_CONV):
        tap = LRU_CONV - 1 - k
        xc = xc + cw[tap:tap + 1, :] * _shift_rows(xe, k)[SUBLANES:, :]
    gates = jnp.dot(xc.astype(BF16), w_ref[...], preferred_element_type=F32) + b_ref[...]
    r = _sigmoid(gates[:, :width])
    i = _sigmoid(gates[:, width:])
    lam = lam_ref[...]
    softplus_neg = jnp.maximum(-lam, 0.0) + jnp.log1p(jnp.exp(-jnp.abs(lam)))
    log_a = (-LRU_C) * r * softplus_neg
    a = jnp.exp(log_a)
    t = jnp.tanh(log_a)
    u = jnp.sqrt(-2.0 * t / (1.0 - t)) * (i * xc)
    run = rows // SUBLANES
    pitch = _run_pitch(run)
    h_slabs = []
    for sl in range(width // LANES):
        cols = slice(sl * LANES, (sl + 1) * LANES)
        _store_runs(a_scr, sl, a[:, cols], run)
        _store_runs(u_scr, sl, u[:, cols], run)
        h_r = a_cum = None
        for r in range(run):
            step = _step_rows(r, run)
            a_r = a_scr[sl, step, :]
            u_r = u_scr[sl, step, :]
            if r == 0:
                h_r, a_cum = u_r, a_r
            else:
                h_r, a_cum = a_r * h_r + u_r, a_r * a_cum
                u_scr[sl, step, :] = h_r
                a_scr[sl, step, :] = a_cum
        p_inc, e_inc = a_cum, h_r
        for sh in (1, 2, 4):
            e_inc, p_inc = (p_inc * _sublane_delay(e_inc, sh, 0.0) + e_inc,
                            p_inc * _sublane_delay(p_inc, sh, 1.0))
        carry = carry_ref[:, cols]
        state_in = _sublane_delay(e_inc, 1, 0.0) + _sublane_delay(p_inc, 1, 1.0) * carry
        carry_ref[:, cols] = (e_inc + p_inc * carry)[SUBLANES - 1:SUBLANES, :]
        pieces = []
        for s in range(SUBLANES):
            blk = slice(s * pitch, s * pitch + run)
            pieces.append(u_scr[sl, blk, :] + a_scr[sl, blk, :] * state_in[s:s + 1, :])
        h_slabs.append(jnp.concatenate(pieces, axis=0))
    h = jnp.concatenate(h_slabs, axis=1)
    out = h * _gelu(g_ref[...])
    ms = jnp.mean(out * out, axis=-1, keepdims=True)
    o_ref[...] = (out * lax.rsqrt(ms + RMS_EPS) * ng_ref[...]).astype(BF16)


def _lru(rest, batch, seq, width, conv_w, conv_b, w_gate_bf, b_gate, lam, norm_g, chunk):
    rest_v = rest.reshape(batch, seq, rest.shape[-1])
    nchunk = seq // chunk
    per8 = chunk // SUBLANES

    def full(shape):
        return pl.BlockSpec(shape, lambda b, n: (0,) * len(shape))

    nb = min(LRU_SEQS_PER_STEP, batch)
    out = pl.pallas_call(
        functools.partial(_lru_kernel, width=width),
        grid=(batch // nb, nchunk),
        in_specs=[
            pl.BlockSpec((nb, chunk, width), lambda b, n: (b, n, 0)),
            pl.BlockSpec((nb, SUBLANES, width), lambda b, n: (b, jnp.maximum(n * per8 - 1, 0), 0)),
            pl.BlockSpec((nb, chunk, width), lambda b, n: (b, n, 1)),
            full((LRU_CONV, width)),
            full((1, width)),
            full((width, 2 * width)),
            full((1, 2 * width)),
            full((1, width)),
            full((1, width)),
        ],
        out_specs=pl.BlockSpec((nb, chunk, width), lambda b, n: (b, n, 0)),
        out_shape=jax.ShapeDtypeStruct((batch, seq, width), BF16),
        scratch_shapes=[
            pltpu.VMEM((nb, 1, width), F32),
            pltpu.VMEM((nb, 2, width // LANES, SUBLANES * _run_pitch(chunk // SUBLANES) + SUBLANES, LANES), F32),
        ],
        compiler_params=_params("parallel", "arbitrary"),
        name="rg_lru",
    )(rest_v, rest_v, rest_v, conv_w, conv_b.reshape(1, width), w_gate_bf,
      b_gate.reshape(1, 2 * width), lam.reshape(1, width), norm_g.reshape(1, width))
    return out.reshape(batch * seq, width)


def _block_diag(w):
    h, i, j = w.shape
    eye = jnp.eye(h, dtype=w.dtype)
    return jnp.einsum('hij,hg->higj', w, eye).reshape(h * i, h * j)


def _cmul(ar, ai, br, bi):
    return ar * br - ai * bi, ar * bi + ai * br


def _s5_kernel(u_ref, bbd_ref, pwr_ref, pwi_ref, mur_ref, mui_ref, cbd_ref, d_ref, wg_ref, bg_ref, ng_ref,
               o_ref, cr_ref, ci_ref, perm_ref, bu_ref, xb_ref, *, nstate):
    n = pl.program_id(1)

    @pl.when(n == 0)
    def _():
        cr_ref[...] = jnp.zeros_like(cr_ref)
        ci_ref[...] = jnp.zeros_like(ci_ref)

    nb, rows, width = u_ref.shape
    run = rows // SUBLANES
    u_run = jnp.concatenate([_s5_to_runs(u_ref.at[b], perm_ref.at[b]) for b in range(nb)], axis=0)
    bu = jnp.dot(u_run.astype(BF16), bbd_ref[...], preferred_element_type=F32)
    bu_ref[...] = bu.reshape(nb, rows, bu.shape[-1])
    for b in range(nb):
        _s5_scan(pwr_ref, pwi_ref, mur_ref, mui_ref, bu_ref.at[b], xb_ref.at[b], cr_ref.at[b], ci_ref.at[b], nstate)
    xb = xb_ref[...].reshape(nb * rows, xb_ref.shape[-1])
    y = jnp.dot(xb, cbd_ref[...], preferred_element_type=F32) + d_ref[...] * u_run
    y = _gelu(y)
    z = jnp.dot(y.astype(BF16), wg_ref[...], preferred_element_type=F32) + bg_ref[...]
    out = y * _sigmoid(z)
    ms = jnp.mean(out * out, axis=-1, keepdims=True)
    out = out * lax.rsqrt(ms + RMS_EPS) * ng_ref[...]
    for b in range(nb):
        slabs = []
        for sl in range(width // LANES):
            for r in range(run):
                step = slice(b * rows + r * SUBLANES, b * rows + (r + 1) * SUBLANES)
                perm_ref[b, sl, _step_rows(r, run), :] = out[step, sl * LANES:(sl + 1) * LANES]
            slabs.append(_load_runs(perm_ref.at[b], sl, run))
        o_ref[b] = jnp.concatenate(slabs, axis=1).astype(BF16)


def _s5_to_runs(u_ref, perm_ref):
    u = u_ref[...]
    rows, width = u.shape
    run = rows // SUBLANES
    slabs = []
    for sl in range(width // LANES):
        _store_runs(perm_ref, sl, u[:, sl * LANES:(sl + 1) * LANES], run)
        slabs.append(jnp.concatenate([perm_ref[sl, _step_rows(r, run), :] for r in range(run)], axis=0))
    return jnp.concatenate(slabs, axis=1)


def _s5_scan(pwr_ref, pwi_ref, mur_ref, mui_ref, bu_ref, xb_ref, cr_ref, ci_ref, nstate):
    rows = bu_ref.shape[0]
    run = rows // SUBLANES
    for j in range(nstate // LANES):
        cols = slice(j * LANES, (j + 1) * LANES)
        re_cols = slice(2 * j * LANES, (2 * j + 1) * LANES)
        im_cols = slice((2 * j + 1) * LANES, (2 * j + 2) * LANES)
        lr = pwr_ref[0:1, cols]
        li = pwi_ref[0:1, cols]
        xr = bu_ref[0:SUBLANES, re_cols]
        xi = bu_ref[0:SUBLANES, im_cols]
        for r in range(1, run):
            step = slice(r * SUBLANES, (r + 1) * SUBLANES)
            dr, di = _cmul(lr, li, xr, xi)
            xr = bu_ref[step, re_cols] + dr
            xi = bu_ref[step, im_cols] + di
            bu_ref[step, re_cols] = xr
            bu_ref[step, im_cols] = xi
        er, ei = xr, xi
        for sh in (1, 2, 4):
            k = run * sh - 1
            dr, di = _cmul(pwr_ref[k:k + 1, cols], pwi_ref[k:k + 1, cols],
                           _sublane_delay(er, sh, 0.0), _sublane_delay(ei, sh, 0.0))
            er, ei = er + dr, ei + di
        cr = cr_ref[:, cols]
        ci = ci_ref[:, cols]
        dr, di = _cmul(mur_ref[:, cols], mui_ref[:, cols], cr, ci)
        in_r = _sublane_delay(er, 1, 0.0) + dr
        in_i = _sublane_delay(ei, 1, 0.0) + di
        dr, di = _cmul(pwr_ref[rows - 1:rows, cols], pwi_ref[rows - 1:rows, cols], cr, ci)
        cr_ref[:, cols] = (er + dr)[SUBLANES - 1:SUBLANES, :]
        ci_ref[:, cols] = (ei + di)[SUBLANES - 1:SUBLANES, :]
        for r0 in range(0, run, 2):
            outs_r, outs_i = [], []
            for r in (r0, r0 + 1):
                step = slice(r * SUBLANES, (r + 1) * SUBLANES)
                dr, di = _cmul(pwr_ref[r:r + 1, cols], pwi_ref[r:r + 1, cols], in_r, in_i)
                outs_r.append(bu_ref[step, re_cols] + dr)
                outs_i.append(bu_ref[step, im_cols] + di)
            pair = slice(r0 * SUBLANES, (r0 + 2) * SUBLANES)
            xb_ref[pair, re_cols] = jnp.concatenate(outs_r, axis=0).astype(BF16)
            xb_ref[pair, im_cols] = jnp.concatenate(outs_i, axis=0).astype(BF16)


def _s5_tables(a_re, a_im, b_re, b_im, c_re, c_im, log_step, chunk):
    g, p = a_re.shape
    step = jnp.exp(log_step)[:, None]
    dt_re, dt_im = step * a_re, step * a_im
    mag = jnp.exp(dt_re)
    ab_re, ab_im = mag * jnp.cos(dt_im), mag * jnp.sin(dt_im)
    z_re, z_im = ab_re - 1.0, ab_im
    den = a_re * a_re + a_im * a_im
    f_re = (z_re * a_re + z_im * a_im) / den
    f_im = (z_im * a_re - z_re * a_im) / den
    bb_re = f_re[..., None] * b_re - f_im[..., None] * b_im
    bb_im = f_re[..., None] * b_im + f_im[..., None] * b_re
    eye = jnp.eye(g, dtype=F32)

    def in_map(bb):
        c = bb.shape[-1]
        return jnp.einsum('gpc,gh->gchp', bb, eye).reshape(g * c, g * p)

    def out_map(cc):
        c = cc.shape[1]
        return jnp.einsum('gcp,gh->gphc', cc, eye).reshape(g * p, g * c)

    n_strip = g * p // LANES
    bbd = jnp.stack([in_map(bb_re).reshape(-1, n_strip, LANES), in_map(bb_im).reshape(-1, n_strip, LANES)],
                    axis=2).reshape(-1, 2 * g * p)
    cbd = jnp.stack([out_map(c_re).reshape(n_strip, LANES, -1), out_map(-c_im).reshape(n_strip, LANES, -1)],
                    axis=1).reshape(2 * g * p, -1)
    pr = ab_re.reshape(1, g * p)
    pi = ab_im.reshape(1, g * p)
    s = 1
    while s < chunk:
        lr, li = pr[s - 1:s], pi[s - 1:s]
        pr, pi = (jnp.concatenate([pr, pr * lr - pi * li], axis=0),
                  jnp.concatenate([pi, pr * li + pi * lr], axis=0))
        s *= 2
    run = chunk // SUBLANES
    run_rows = jnp.arange(1, SUBLANES) * run - 1
    mur = jnp.concatenate([jnp.ones((1, g * p), F32), pr[run_rows]], axis=0)
    mui = jnp.concatenate([jnp.zeros((1, g * p), F32), pi[run_rows]], axis=0)
    return bbd.astype(BF16), pr, pi, mur, mui, cbd.astype(BF16)


def _s5(rest, batch, seq, col_block, width, tables, d, w_glu_bf, b_glu, norm_g, chunk):
    bbd, pwr, pwi, mur, mui, cbd = tables
    nstate = pwr.shape[1]
    rest_v = rest.reshape(batch, seq, rest.shape[-1])

    def full(shape):
        return pl.BlockSpec(shape, lambda b, n: (0,) * len(shape))

    nb = min(S5_SEQS_PER_STEP, batch)
    out = pl.pallas_call(
        functools.partial(_s5_kernel, nstate=nstate),
        grid=(batch // nb, seq // chunk),
        in_specs=[
            pl.BlockSpec((nb, chunk, width), lambda b, n: (b, n, col_block)),
            full((width, 2 * nstate)),
            full((chunk, nstate)),
            full((chunk, nstate)),
            full((SUBLANES, nstate)),
            full((SUBLANES, nstate)),
            full((2 * nstate, width)),
            full((1, width)),
            full((width, width)),
            full((1, width)),
            full((1, width)),
        ],
        out_specs=pl.BlockSpec((nb, chunk, width), lambda b, n: (b, n, 0)),
        out_shape=jax.ShapeDtypeStruct((batch, seq, width), BF16),
        scratch_shapes=[pltpu.VMEM((nb, 1, nstate), F32)] * 2 + [
            pltpu.VMEM((nb, width // LANES, SUBLANES * _run_pitch(chunk // SUBLANES), LANES), F32),
            pltpu.VMEM((nb, chunk, 2 * nstate), F32),
            pltpu.VMEM((nb, chunk, 2 * nstate), BF16),
        ],
        compiler_params=_params("parallel", "arbitrary"),
        name="s5_scan",
    )(rest_v, bbd, pwr, pwi, mur, mui, cbd, d.reshape(1, width), w_glu_bf, b_glu.reshape(1, width),
      norm_g.reshape(1, width))
    return out.reshape(batch * seq, width)


def _layer_norm_rows(y, g, b):
    mu = jnp.mean(y, axis=-1, keepdims=True)
    yc = y - mu
    var = jnp.mean(yc * yc, axis=-1, keepdims=True)
    return yc * lax.rsqrt(var + LN_EPS) * g + b


def _mix_kernel(*refs, dilations):
    nb = len(dilations)
    o_refs, l_refs = refs[:nb], refs[nb:2 * nb]
    lru_ref, s5_ref, h_ref, w_ref, ga_ref, lg_ref, lb_ref, out_ref, tok_ref = refs[2 * nb:]
    npair = o_refs[0].shape[1]
    tm = h_ref.shape[0]

    part = tm // MIX_ROW_PARTS

    def token_major(ref, d, slot, p, r0):
        if d == 1:
            return ref[0, p, r0:r0 + part, :].astype(F32)
        for c in range(d):
            tok_ref[slot, pl.ds(r0 + c, part // d, stride=d), :] = (
                ref[c, p, r0 // d:(r0 + part) // d, :].astype(F32))
        return tok_ref[slot, r0:r0 + part, :]

    ga = ga_ref[...]
    for part_idx in range(MIX_ROW_PARTS):
        r0 = part_idx * part
        rows = slice(r0, r0 + part)
        attn = []
        ssq = jnp.zeros((part, 1), F32)
        for p in range(npair):
            outs = [token_major(o_refs[i], d, 2 * (nb * p + i), p, r0) for i, d in enumerate(dilations)]
            lses = [token_major(l_refs[i], d, 2 * (nb * p + i) + 1, p, r0) for i, d in enumerate(dilations)]
            m = functools.reduce(jnp.maximum, lses)
            es = [jnp.exp(l - m) for l in lses]
            a = sum(e * o for e, o in zip(es, outs)) / sum(es)
            ssq = ssq + jnp.sum(a * a, axis=-1, keepdims=True)
            attn.append(a)
        scale = lax.rsqrt(ssq * (1.0 / (npair * HEAD_PAIR)) + RMS_EPS)
        attn_n = [(a * scale * ga[:, p * HEAD_PAIR:(p + 1) * HEAD_PAIR]).astype(BF16) for p, a in enumerate(attn)]
        mixed = jnp.concatenate(attn_n + [lru_ref[rows, :], s5_ref[rows, :]], axis=1)
        mix = jnp.dot(mixed, w_ref[...], preferred_element_type=F32)
        y = DEEPNORM_ALPHA * h_ref[rows, :] + mix
        out_ref[rows, :] = _layer_norm_rows(y, lg_ref[...], lb_ref[...])


def _mix(attn_outs, dilations, lru, s5, h, seq, w_out_bf, g_attn, ln_g, ln_b, tm):
    t, d = h.shape
    aw = lru.shape[1]
    sw = s5.shape[1]
    nseq = seq // tm
    npair = aw // HEAD_PAIR

    def rows(w):
        return pl.BlockSpec((tm, w), lambda i: (i, 0))

    def full(shape):
        return pl.BlockSpec(shape, lambda i: (0,) * len(shape))

    def class_major(dil):
        return pl.BlockSpec((None, dil, npair, tm // dil, HEAD_PAIR),
                            lambda i: (i // nseq, 0, 0, i % nseq, 0))

    branch_specs = [class_major(dil) for dil in dilations]
    return pl.pallas_call(
        functools.partial(_mix_kernel, dilations=tuple(dilations)),
        grid=(t // tm,),
        in_specs=branch_specs + branch_specs + [rows(aw), rows(sw), rows(d), full((d, d)),
                                                full((1, aw)), full((1, d)), full((1, d))],
        out_specs=rows(d),
        out_shape=jax.ShapeDtypeStruct((t, d), F32),
        scratch_shapes=[pltpu.VMEM((2 * len(dilations) * npair, tm + SUBLANES, HEAD_PAIR), F32)],
        compiler_params=_params("parallel"),
        name="mix_out_ln",
    )(*[o for o, _ in attn_outs], *[l for _, l in attn_outs], lru, s5, h, w_out_bf,
      g_attn.reshape(1, aw), ln_g.reshape(1, d), ln_b.reshape(1, d))


def _ffn_kernel(h_ref, hp_ref, wu_ref, cw_ref, cb_ref, wd_ref, lg_ref, lb_ref, out_ref,
                *, tiles_per_seq, ff_chunk):
    i = pl.program_id(0)
    keep_prev = jnp.where(i % tiles_per_seq == 0, 0.0, 1.0).astype(F32)
    h = h_ref[...]
    he = jnp.concatenate([hp_ref[...] * keep_prev, h], axis=0).astype(BF16)
    d_ff = wd_ref.shape[0]

    def conv_branch(col0):
        cols = slice(col0, col0 + ff_chunk)
        up = jnp.dot(he, wu_ref[:, cols], preferred_element_type=F32)
        cw = cw_ref[:, cols]
        y = cw[FFN_CONV - 1:FFN_CONV, :] * up
        for k in range(1, FFN_CONV):
            tap = FFN_CONV - 1 - k
            y = y + cw[tap:tap + 1, :] * _shift_rows(up, k)
        return y[SUBLANES:, :] + cb_ref[:, cols]

    acc = None
    for j in range(d_ff // ff_chunk):
        gate = conv_branch(j * ff_chunk)
        val = conv_branch(d_ff + j * ff_chunk)
        act = (_gelu(gate) * val).astype(BF16)
        part = jnp.dot(act, wd_ref[j * ff_chunk:(j + 1) * ff_chunk, :], preferred_element_type=F32)
        acc = part if acc is None else acc + part
    y = DEEPNORM_ALPHA * h + acc
    out_ref[...] = _layer_norm_rows(y, lg_ref[...], lb_ref[...])


def _ffn(h, seq, w_up_bf, conv_w, conv_b, w_down_bf, ln_g, ln_b, tm, cw):
    t, d = h.shape
    d_ff = w_down_bf.shape[0]
    per8 = tm // SUBLANES

    def resident(shape):
        return pl.BlockSpec(shape, lambda i: (0,) * len(shape), pipeline_mode=pl.Buffered(1))

    return pl.pallas_call(
        functools.partial(_ffn_kernel, tiles_per_seq=seq // tm, ff_chunk=cw),
        grid=(t // tm,),
        in_specs=[
            pl.BlockSpec((tm, d), lambda i: (i, 0)),
            pl.BlockSpec((SUBLANES, d), lambda i: (jnp.maximum(i * per8 - 1, 0), 0)),
            resident((d, 2 * d_ff)),
            resident((FFN_CONV, 2 * d_ff)),
            resident((1, 2 * d_ff)),
            resident((d_ff, d)),
            resident((1, d)),
            resident((1, d)),
        ],
        out_specs=pl.BlockSpec((tm, d), lambda i: (i, 0)),
        out_shape=jax.ShapeDtypeStruct((t, d), F32),
        compiler_params=pltpu.CompilerParams(dimension_semantics=("parallel",),
                                             vmem_limit_bytes=FFN_VMEM_LIMIT),
        name="conv_ffn_ln",
    )(h, h, w_up_bf, conv_w, conv_b.reshape(1, 2 * d_ff), w_down_bf,
      ln_g.reshape(1, d), ln_b.reshape(1, d))


def kernel(x, w_in, lru_conv_w, lru_conv_b, lru_wr, lru_br, lru_wi, lru_bi, lru_lambda, s5_a_re, s5_a_im, s5_b_re, s5_b_im, s5_c_re, s5_c_im, s5_d, s5_log_step, s5_w_glu, s5_b_glu, mix_norm_g, w_out, ln1_g, ln1_b, w_up, ffn_conv_w, ffn_conv_b, w_down, ln2_g, ln2_b):
    batch, seq, d_model = x.shape
    depth = w_in.shape[0]
    lru_width = lru_conv_w.shape[-1]
    s5_width = s5_w_glu.shape[-1]
    attn_width = (w_in.shape[-1] - 2 * lru_width - s5_width) // 3
    assert attn_width % HEAD_PAIR == 0 and lru_width == attn_width
    assert all(w // d == ATTN_BLOCK for w, d in DILATED_PAIRS)
    dilations = [d for _, d in DILATED_PAIRS]
    s5_col_block = (2 * lru_width) // s5_width
    assert s5_col_block * s5_width == 2 * lru_width

    cos_t, sin_t = _rope_tables(seq)
    all_tables = jax.vmap(functools.partial(_s5_tables, chunk=S5_CHUNK))(
        s5_a_re, s5_a_im, s5_b_re, s5_b_im, s5_c_re, s5_c_im, s5_log_step)
    all_w_gate = jnp.concatenate([jax.vmap(_block_diag)(lru_wr), jax.vmap(_block_diag)(lru_wi)], axis=2).astype(BF16)
    all_b_gate = jnp.concatenate([lru_br, lru_bi], axis=1)
    h = x.reshape(batch * seq, d_model)
    for l in range(depth):
        qkvs, rest = _proj(h, w_in[l].astype(BF16), cos_t, sin_t, batch, attn_width, dilations, PROJ_TILE)
        attn_outs = [_attn_branch(qkv_cm, d) for qkv_cm, d in zip(qkvs, dilations)]
        lru = _lru(rest, batch, seq, lru_width, lru_conv_w[l], lru_conv_b[l], all_w_gate[l], all_b_gate[l],
                   lru_lambda[l], mix_norm_g[l, attn_width:attn_width + lru_width], LRU_CHUNK)
        tables = tuple(t[l] for t in all_tables)
        s5 = _s5(rest, batch, seq, s5_col_block, s5_width, tables, s5_d[l].reshape(-1),
                 s5_w_glu[l].astype(BF16), s5_b_glu[l], mix_norm_g[l, attn_width + lru_width:], S5_CHUNK)
        h1 = _mix(attn_outs, dilations, lru, s5, h, seq, w_out[l].astype(BF16), mix_norm_g[l, :attn_width],
                  ln1_g[l], ln1_b[l], MIX_TILE)
        h = _ffn(h1, seq, w_up[l].astype(BF16), ffn_conv_w[l], ffn_conv_b[l], w_down[l].astype(BF16),
                 ln2_g[l], ln2_b[l], FFN_TILE, FFN_CHUNK)
    return h.reshape(batch, seq, d_model)
```

```python
import functools

import jax
import jax.numpy as jnp
from jax import lax
from jax.experimental import pallas as pl
from jax.experimental.pallas import tpu as pltpu

F32 = jnp.float32
BF16 = jnp.bfloat16

HEAD_DIM = 64
HEAD_PAIR = 2 * HEAD_DIM
DILATED_PAIRS = ((128, 1), (512, 4), (2048, 16))
ATTN_BLOCK = 128
ROPE_THETA = 10000.0
LRU_CONV = 4
LRU_C = 8.0
FFN_CONV = 3
DEPTH = 2
DEEPNORM_ALPHA = (2 * DEPTH) ** 0.25
LN_EPS = 1e-5
RMS_EPS = 1e-6
NEG_BIG = -1e30
LN2 = 0.6931471805599453
Q_SCALE = HEAD_DIM ** -0.5 / LN2

LANES = 128
SUBLANES = 8
VMEM_LIMIT = 48 * 1024 * 1024
FFN_VMEM_LIMIT = 56 * 1024 * 1024

PROJ_TILE = 1024
MIX_TILE = 1024
MIX_ROW_PARTS = 4
FFN_TILE = 512
FFN_CHUNK = 1536
LRU_CHUNK = 256
LRU_SEQS_PER_STEP = 4
S5_CHUNK = 256
S5_SEQS_PER_STEP = 4
ATTN_QUERIES_PER_STEP = 1024
ATTN_PV_LAG = 2


def _params(*sem):
    return pltpu.CompilerParams(dimension_semantics=sem, vmem_limit_bytes=VMEM_LIMIT)


def _gelu(x):
    return jax.nn.gelu(x)


def _sigmoid(x):
    return 0.5 * jnp.tanh(0.5 * x) + 0.5


def _shift_rows(x, s):
    return pltpu.roll(x, s, 0)


def _run_pitch(run):
    return run + SUBLANES


def _store_runs(scr_ref, slab, x, run):
    pitch = _run_pitch(run)
    for s in range(SUBLANES):
        scr_ref[slab, s * pitch:s * pitch + run, :] = x[s * run:(s + 1) * run, :]


def _load_runs(scr_ref, slab, run):
    pitch = _run_pitch(run)
    return jnp.concatenate([scr_ref[slab, s * pitch:s * pitch + run, :] for s in range(SUBLANES)], axis=0)


def _step_rows(r, run):
    return pl.ds(r, SUBLANES, stride=_run_pitch(run))


def _sublane_delay(v, s, fill):
    sub = lax.broadcasted_iota(jnp.int32, v.shape, 0)
    return jnp.where(sub >= s, pltpu.roll(v, s, 0), fill)


def _proj_kernel(h_ref, w_ref, cos_ref, sin_ref, *refs, attn_width, dilations):
    out_refs = refs[:len(dilations)]
    rest_ref, stage_ref, restage_ref = refs[len(dilations):]
    n_slab = 3 * attn_width // LANES
    n_rot = 2 * attn_width // LANES
    npair = attn_width // HEAD_PAIR
    rows = h_ref.shape[0]
    acc = jnp.dot(h_ref[...].astype(BF16), w_ref[...], preferred_element_type=F32)
    cos = cos_ref[...]
    sin = sin_ref[...]
    lane = lax.broadcasted_iota(jnp.int32, cos.shape, 1)
    first_half = (lane & (HEAD_DIM // 2)) == 0
    for cb in range(n_slab):
        x = acc[:, cb * LANES:(cb + 1) * LANES]
        if cb < n_rot:
            partner = jnp.where(first_half, pltpu.roll(x, LANES - HEAD_DIM // 2, 1),
                                pltpu.roll(x, HEAD_DIM // 2, 1))
            x = x * cos + partner * sin
        if cb < n_rot // 2:
            x = x * Q_SCALE
        stage_ref[cb, 0:rows, :] = x
    rest_ref[...] = acc[:, 3 * attn_width:]
    src_ref, src_d = stage_ref, 1
    for idx, (out_ref, d) in enumerate(zip(out_refs, dilations)):
        assert d % src_d == 0
        ratio = d // src_d
        keep = d > 1 and idx + 1 < len(dilations) and dilations[idx + 1] % d == 0
        for c in range(d):
            base = (c % src_d) * (rows // src_d) + c // src_d
            cls_rows = slice(c * (rows // d), (c + 1) * (rows // d))
            for cb in range(n_slab):
                if ratio == 1:
                    piece = src_ref[cb, cls_rows, :]
                else:
                    piece = src_ref[cb, pl.ds(base, rows // d, stride=ratio), :]
                if keep:
                    restage_ref[cb, cls_rows, :] = piece
                out_ref[c, 3 * (cb % npair) + cb // npair] = piece.astype(BF16)
        if keep:
            src_ref, src_d = restage_ref, d


def _rope_tables(seq):
    half = HEAD_DIM // 2
    pos = jnp.arange(seq, dtype=F32)
    inv = ROPE_THETA ** (-jnp.arange(half, dtype=F32) * 2.0 / HEAD_DIM)
    ang = pos[:, None] * inv[None, :]
    cos = jnp.cos(ang)
    sin = jnp.sin(ang)
    cos_t = jnp.concatenate([cos, cos, cos, cos], axis=1)
    sin_t = jnp.concatenate([-sin, sin, -sin, sin], axis=1)
    return cos_t, sin_t


def _proj(h, w_in_bf, cos_t, sin_t, batch, attn_width, dilations, tm):
    t, d = h.shape
    d_in = w_in_bf.shape[1]
    seq = cos_t.shape[0]
    nseq = seq // tm
    rest_w = d_in - 3 * attn_width
    n_slab = 3 * attn_width // LANES
    qkv_specs = [pl.BlockSpec((None, dil, n_slab, tm // dil, LANES),
                              lambda i: (i // nseq, 0, 0, i % nseq, 0)) for dil in dilations]
    qkv_shapes = [jax.ShapeDtypeStruct((batch, dil, n_slab, seq // dil, LANES), BF16) for dil in dilations]
    outs = pl.pallas_call(
        functools.partial(_proj_kernel, attn_width=attn_width, dilations=tuple(dilations)),
        grid=(t // tm,),
        in_specs=[
            pl.BlockSpec((tm, d), lambda i: (i, 0)),
            pl.BlockSpec((d, d_in), lambda i: (0, 0)),
            pl.BlockSpec((tm, LANES), lambda i: (i % nseq, 0)),
            pl.BlockSpec((tm, LANES), lambda i: (i % nseq, 0)),
        ],
        out_specs=qkv_specs + [pl.BlockSpec((tm, rest_w), lambda i: (i, 0))],
        out_shape=qkv_shapes + [jax.ShapeDtypeStruct((t, rest_w), F32)],
        scratch_shapes=[pltpu.VMEM((n_slab, tm + SUBLANES, LANES), F32)] * 2,
        compiler_params=_params("parallel"),
        name="proj_rope",
    )(h, w_in_bf, cos_t, sin_t)
    return outs[:-1], outs[-1]


def _attn_kernel(qkv_ref, *rest, has_prev):
    if has_prev:
        prev_ref, o_ref, l_ref = rest
    else:
        o_ref, l_ref = rest
    ncls, _, qb, _ = qkv_ref.shape
    q_slab, k_slab, v_slab = range(3)
    win = (ATTN_BLOCK, 2 * ATTN_BLOCK)
    row = lax.broadcasted_iota(jnp.int32, win, 0)
    col = lax.broadcasted_iota(jnp.int32, win, 1)
    band_bias = jnp.where((col >= row) & (col <= row + ATTN_BLOCK), 0.0, NEG_BIG).astype(F32)
    if has_prev:
        no_prev = jnp.where(pl.program_id(3) > 0, 0.0, NEG_BIG).astype(F32)
        first_bias = band_bias + jnp.where(col < ATTN_BLOCK, no_prev, 0.0)
    else:
        first_bias = jnp.where(col < ATTN_BLOCK, NEG_BIG, band_bias).astype(F32)
    head0_q = lax.broadcasted_iota(jnp.int32, (ATTN_BLOCK, HEAD_PAIR), 1) < HEAD_DIM
    head0_kv = lax.broadcasted_iota(jnp.int32, (2 * ATTN_BLOCK, HEAD_PAIR), 1) < HEAD_DIM
    contract_last = (((1,), (1,)), ((), ()))
    items = [(cls, j, hd) for cls in range(ncls) for j in range(qb // ATTN_BLOCK) for hd in range(2)]

    def window(slab, cls, j):
        rows = slice(j * ATTN_BLOCK, (j + 1) * ATTN_BLOCK)
        if j > 0:
            return qkv_ref[cls, slab, (j - 1) * ATTN_BLOCK:(j + 1) * ATTN_BLOCK, :]
        if has_prev:
            return jnp.concatenate([prev_ref[cls, slab], qkv_ref[cls, slab, rows, :]], axis=0)
        return jnp.concatenate([qkv_ref[cls, slab, rows, :]] * 2, axis=0)

    def scores(cls, j, hd):
        q = qkv_ref[cls, q_slab, j * ATTN_BLOCK:(j + 1) * ATTN_BLOCK, :]
        sel_q = head0_q if hd == 0 else jnp.logical_not(head0_q)
        qh = jnp.where(sel_q, q, jnp.zeros_like(q))
        kw = window(k_slab, cls, j)
        bias = band_bias if j > 0 else first_bias
        return lax.dot_general(qh, kw, contract_last, preferred_element_type=F32) + bias

    def softmax_numerators(s):
        m = jnp.max(s, axis=1, keepdims=True)
        return jnp.exp2(s - m).astype(BF16), m

    def weighted_values(p, cls, j, hd):
        vw = window(v_slab, cls, j)
        sel_kv = head0_kv if hd == 0 else jnp.logical_not(head0_kv)
        v_aug = jnp.where(sel_kv, vw, jnp.ones_like(vw))
        return jnp.dot(p, v_aug, preferred_element_type=F32)

    def finish(cls, j, pvs, maxes):
        rows = slice(j * ATTN_BLOCK, (j + 1) * ATTN_BLOCK)
        num = jnp.where(head0_q, pvs[0], pvs[1])
        den = pltpu.roll(jnp.where(head0_q, pvs[1], pvs[0]), HEAD_DIM, 1)
        o_ref[cls, rows, :] = (num / den).astype(o_ref.dtype)
        l_ref[cls, rows, :] = (jnp.where(head0_q, maxes[0], maxes[1]) + jnp.log2(den)) * LN2

    s_vals, p_vals, m_vals, pv_vals = {}, {}, {}, {}
    for t in range(len(items) + ATTN_PV_LAG):
        if t < len(items):
            s_vals[t] = scores(*items[t])
        if 1 <= t <= len(items):
            p_vals[t - 1], m_vals[t - 1] = softmax_numerators(s_vals.pop(t - 1))
        if t >= ATTN_PV_LAG:
            i = t - ATTN_PV_LAG
            cls, j, hd = items[i]
            pv_vals[i] = weighted_values(p_vals.pop(i), cls, j, hd)
            if hd == 1:
                finish(cls, j, [pv_vals.pop(i - 1), pv_vals.pop(i)], [m_vals.pop(i - 1), m_vals.pop(i)])


def _attn_branch(qkv_cm, dilation):
    batch, _, n_slab, m, _ = qkv_cm.shape
    npair = n_slab // 3
    qb = min(ATTN_QUERIES_PER_STEP, m)
    ncls = min(ATTN_QUERIES_PER_STEP // qb, dilation)
    nsub = qb // ATTN_BLOCK
    nblk = m // qb
    has_prev = nblk > 1

    own = pl.BlockSpec((None, ncls, 3, qb, HEAD_PAIR), lambda b, c, p, n: (b, c, p, n, 0))
    prev = pl.BlockSpec((None, ncls, 3, ATTN_BLOCK, HEAD_PAIR),
                        lambda b, c, p, n: (b, c, p, jnp.maximum(n * nsub - 1, 0), 0))
    out_spec = pl.BlockSpec((None, ncls, None, qb, HEAD_PAIR), lambda b, c, p, n: (b, c, p, n, 0))
    out_shapes = [jax.ShapeDtypeStruct((batch, dilation, npair, m, HEAD_PAIR), dt) for dt in (BF16, F32)]
    in_specs = [own] + ([prev] if has_prev else [])
    return pl.pallas_call(
        functools.partial(_attn_kernel, has_prev=has_prev),
        grid=(batch, dilation // ncls, npair, nblk),
        in_specs=in_specs,
        out_specs=[out_spec, out_spec],
        out_shape=out_shapes,
        compiler_params=_params("parallel", "parallel", "parallel", "arbitrary"),
        name=f"dilated_attn_d{dilation}",
    )(*([qkv_cm] * len(in_specs)))


def _lru_kernel(x_ref, xp_ref, g_ref, cw_ref, cb_ref, w_ref, b_ref, lam_ref, ng_ref,
                o_ref, carry_ref, au_scr, *, width):
    n = pl.program_id(1)

    @pl.when(n == 0)
    def _():
        carry_ref[...] = jnp.zeros_like(carry_ref)

    for b in range(x_ref.shape[0]):
        _lru_sequence(n, x_ref.at[b], xp_ref.at[b], g_ref.at[b], cw_ref, cb_ref, w_ref, b_ref, lam_ref, ng_ref,
                      o_ref.at[b], carry_ref.at[b], au_scr.at[b, 0], au_scr.at[b, 1], width)


def _lru_sequence(n, x_ref, xp_ref, g_ref, cw_ref, cb_ref, w_ref, b_ref, lam_ref, ng_ref,
                  o_ref, carry_ref, a_scr, u_scr, width):
    x = x_ref[...]
    rows = x.shape[0]
    xp = xp_ref[...] * jnp.where(n > 0, 1.0, 0.0).astype(F32)
    xe = jnp.concatenate([xp, x], axis=0)
    cw = cw_ref[...]
    xc = cw[LRU_CONV - 1:LRU_CONV, :] * x + cb_ref[...]
    for k in range(1, LRU_CONV):
        tap = LRU_CONV - 1 - k
        xc = xc + cw[tap:tap + 1, :] * _shift_rows(xe, k)[SUBLANES:, :]
    gates = jnp.dot(xc.astype(BF16), w_ref[...], preferred_element_type=F32) + b_ref[...]
    r = _sigmoid(gates[:, :width])
    i = _sigmoid(gates[:, width:])
    lam = lam_ref[...]
    softplus_neg = jnp.maximum(-lam, 0.0) + jnp.log1p(jnp.exp(-jnp.abs(lam)))
    log_a = (-LRU_C) * r * softplus_neg
    a = jnp.exp(log_a)
    t = jnp.tanh(log_a)
    u = jnp.sqrt(-2.0 * t / (1.0 - t)) * (i * xc)
    run = rows // SUBLANES
    pitch = _run_pitch(run)
    h_slabs = []
    for sl in range(width // LANES):
        cols = slice(sl * LANES, (sl + 1) * LANES)
        _store_runs(a_scr, sl, a[:, cols], run)
        _store_runs(u_scr, sl, u[:, cols], run)
        h_r = a_cum = None
        for r in range(run):
            step = _step_rows(r, run)
            a_r = a_scr[sl, step, :]
            u_r = u_scr[sl, step, :]
            if r == 0:
                h_r, a_cum = u_r, a_r
            else:
                h_r, a_cum = a_r * h_r + u_r, a_r * a_cum
                u_scr[sl, step, :] = h_r
                a_scr[sl, step, :] = a_cum
        p_inc, e_inc = a_cum, h_r
        for sh in (1, 2, 4):
            e_inc, p_inc = (p_inc * _sublane_delay(e_inc, sh, 0.0) + e_inc,
                            p_inc * _sublane_delay(p_inc, sh, 1.0))
        carry = carry_ref[:, cols]
        state_in = _sublane_delay(e_inc, 1, 0.0) + _sublane_delay(p_inc, 1, 1.0) * carry
        carry_ref[:, cols] = (e_inc + p_inc * carry)[SUBLANES - 1:SUBLANES, :]
        pieces = []
        for s in range(SUBLANES):
            blk = slice(s * pitch, s * pitch + run)
            pieces.append(u_scr[sl, blk, :] + a_scr[sl, blk, :] * state_in[s:s + 1, :])
        h_slabs.append(jnp.concatenate(pieces, axis=0))
    h = jnp.concatenate(h_slabs, axis=1)
    out = h * _gelu(g_ref[...])
    ms = jnp.mean(out * out, axis=-1, keepdims=True)
    o_ref[...] = (out * lax.rsqrt(ms + RMS_EPS) * ng_ref[...]).astype(BF16)


def _lru(rest, batch, seq, width, conv_w, conv_b, w_gate_bf, b_gate, lam, norm_g, chunk):
    rest_v = rest.reshape(batch, seq, rest.shape[-1])
    nchunk = seq // chunk
    per8 = chunk // SUBLANES

    def full(shape):
        return pl.BlockSpec(shape, lambda b, n: (0,) * len(shape))

    nb = min(LRU_SEQS_PER_STEP, batch)
    out = pl.pallas_call(
        functools.partial(_lru_kernel, width=width),
        grid=(batch // nb, nchunk),
        in_specs=[
            pl.BlockSpec((nb, chunk, width), lambda b, n: (b, n, 0)),
            pl.BlockSpec((nb, SUBLANES, width), lambda b, n: (b, jnp.maximum(n * per8 - 1, 0), 0)),
            pl.BlockSpec((nb, chunk, width), lambda b, n: (b, n, 1)),
            full((LRU_CONV, width)),
            full((1, width)),
            full((width, 2 * width)),
            full((1, 2 * width)),
            full((1, width)),
            full((1, width)),
        ],
        out_specs=pl.BlockSpec((nb, chunk, width), lambda b, n: (b, n, 0)),
        out_shape=jax.ShapeDtypeStruct((batch, seq, width), BF16),
        scratch_shapes=[
            pltpu.VMEM((nb, 1, width), F32),
            pltpu.VMEM((nb, 2, width // LANES, SUBLANES * _run_pitch(chunk // SUBLANES) + SUBLANES, LANES), F32),
        ],
        compiler_params=_params("parallel", "arbitrary"),
        name="rg_lru",
    )(rest_v, rest_v, rest_v, conv_w, conv_b.reshape(1, width), w_gate_bf,
      b_gate.reshape(1, 2 * width), lam.reshape(1, width), norm_g.reshape(1, width))
    return out.reshape(batch * seq, width)


def _block_diag(w):
    h, i, j = w.shape
    eye = jnp.eye(h, dtype=w.dtype)
    return jnp.einsum('hij,hg->higj', w, eye).reshape(h * i, h * j)


def _cmul(ar, ai, br, bi):
    return ar * br - ai * bi, ar * bi + ai * br


def _s5_kernel(u_ref, bbd_ref, pwr_ref, pwi_ref, mur_ref, mui_ref, cbd_ref, d_ref, wg_ref, bg_ref, ng_ref,
               o_ref, cr_ref, ci_ref, perm_ref, bu_ref, xb_ref, *, nstate):
    n = pl.program_id(1)

    @pl.when(n == 0)
    def _():
        cr_ref[...] = jnp.zeros_like(cr_ref)
        ci_ref[...] = jnp.zeros_like(ci_ref)

    nb, rows, width = u_ref.shape
    run = rows // SUBLANES
    u_run = jnp.concatenate([_s5_to_runs(u_ref.at[b], perm_ref.at[b]) for b in range(nb)], axis=0)
    bu = jnp.dot(u_run.astype(BF16), bbd_ref[...], preferred_element_type=F32)
    bu_ref[...] = bu.reshape(nb, rows, bu.shape[-1])
    for b in range(nb):
        _s5_scan(pwr_ref, pwi_ref, mur_ref, mui_ref, bu_ref.at[b], xb_ref.at[b], cr_ref.at[b], ci_ref.at[b], nstate)
    xb = xb_ref[...].reshape(nb * rows, xb_ref.shape[-1])
    y = jnp.dot(xb, cbd_ref[...], preferred_element_type=F32) + d_ref[...] * u_run
    y = _gelu(y)
    z = jnp.dot(y.astype(BF16), wg_ref[...], preferred_element_type=F32) + bg_ref[...]
    out = y * _sigmoid(z)
    ms = jnp.mean(out * out, axis=-1, keepdims=True)
    out = out * lax.rsqrt(ms + RMS_EPS) * ng_ref[...]
    for b in range(nb):
        slabs = []
        for sl in range(width // LANES):
            for r in range(run):
                step = slice(b * rows + r * SUBLANES, b * rows + (r + 1) * SUBLANES)
                perm_ref[b, sl, _step_rows(r, run), :] = out[step, sl * LANES:(sl + 1) * LANES]
            slabs.append(_load_runs(perm_ref.at[b], sl, run))
        o_ref[b] = jnp.concatenate(slabs, axis=1).astype(BF16)


def _s5_to_runs(u_ref, perm_ref):
    u = u_ref[...]
    rows, width = u.shape
    run = rows // SUBLANES
    slabs = []
    for sl in range(width // LANES):
        _store_runs(perm_ref, sl, u[:, sl * LANES:(sl + 1) * LANES], run)
        slabs.append(jnp.concatenate([perm_ref[sl, _step_rows(r, run), :] for r in range(run)], axis=0))
    return jnp.concatenate(slabs, axis=1)


def _s5_scan(pwr_ref, pwi_ref, mur_ref, mui_ref, bu_ref, xb_ref, cr_ref, ci_ref, nstate):
    rows = bu_ref.shape[0]
    run = rows // SUBLANES
    for j in range(nstate // LANES):
        cols = slice(j * LANES, (j + 1) * LANES)
        re_cols = slice(2 * j * LANES, (2 * j + 1) * LANES)
        im_cols = slice((2 * j + 1) * LANES, (2 * j + 2) * LANES)
        lr = pwr_ref[0:1, cols]
        li = pwi_ref[0:1, cols]
        xr = bu_ref[0:SUBLANES, re_cols]
        xi = bu_ref[0:SUBLANES, im_cols]
        for r in range(1, run):
            step = slice(r * SUBLANES, (r + 1) * SUBLANES)
            dr, di = _cmul(lr, li, xr, xi)
            xr = bu_ref[step, re_cols] + dr
            xi = bu_ref[step, im_cols] + di
            bu_ref[step, re_cols] = xr
            bu_ref[step, im_cols] = xi
        er, ei = xr, xi
        for sh in (1, 2, 4):
            k = run * sh - 1
            dr, di = _cmul(pwr_ref[k:k + 1, cols], pwi_ref[k:k + 1, cols],
                           _sublane_delay(er, sh, 0.0), _sublane_delay(ei, sh, 0.0))
            er, ei = er + dr, ei + di
        cr = cr_ref[:, cols]
        ci = ci_ref[:, cols]
        dr, di = _cmul(mur_ref[:, cols], mui_ref[:, cols], cr, ci)
        in_r = _sublane_delay(er, 1, 0.0) + dr
        in_i = _sublane_delay(ei, 1, 0.0) + di
        dr, di = _cmul(pwr_ref[rows - 1:rows, cols], pwi_ref[rows - 1:rows, cols], cr, ci)
        cr_ref[:, cols] = (er + dr)[SUBLANES - 1:SUBLANES, :]
        ci_ref[:, cols] = (ei + di)[SUBLANES - 1:SUBLANES, :]
        for r0 in range(0, run, 2):
            outs_r, outs_i = [], []
            for r in (r0, r0 + 1):
                step = slice(r * SUBLANES, (r + 1) * SUBLANES)
                dr, di = _cmul(pwr_ref[r:r + 1, cols], pwi_ref[r:r + 1, cols], in_r, in_i)
                outs_r.append(bu_ref[step, re_cols] + dr)
                outs_i.append(bu_ref[step, im_cols] + di)
            pair = slice(r0 * SUBLANES, (r0 + 2) * SUBLANES)
            xb_ref[pair, re_cols] = jnp.concatenate(outs_r, axis=0).astype(BF16)
            xb_ref[pair, im_cols] = jnp.concatenate(outs_i, axis=0).astype(BF16)


def _s5_tables(a_re, a_im, b_re, b_im, c_re, c_im, log_step, chunk):
    g, p = a_re.shape
    step = jnp.exp(log_step)[:, None]
    dt_re, dt_im = step * a_re, step * a_im
    mag = jnp.exp(dt_re)
    ab_re, ab_im = mag * jnp.cos(dt_im), mag * jnp.sin(dt_im)
    z_re, z_im = ab_re - 1.0, ab_im
    den = a_re * a_re + a_im * a_im
    f_re = (z_re * a_re + z_im * a_im) / den
    f_im = (z_im * a_re - z_re * a_im) / den
    bb_re = f_re[..., None] * b_re - f_im[..., None] * b_im
    bb_im = f_re[..., None] * b_im + f_im[..., None] * b_re
    eye = jnp.eye(g, dtype=F32)

    def in_map(bb):
        c = bb.shape[-1]
        return jnp.einsum('gpc,gh->gchp', bb, eye).reshape(g * c, g * p)

    def out_map(cc):
        c = cc.shape[1]
        return jnp.einsum('gcp,gh->gphc', cc, eye).reshape(g * p, g * c)

    n_strip = g * p // LANES
    bbd = jnp.stack([in_map(bb_re).reshape(-1, n_strip, LANES), in_map(bb_im).reshape(-1, n_strip, LANES)],
                    axis=2).reshape(-1, 2 * g * p)
    cbd = jnp.stack([out_map(c_re).reshape(n_strip, LANES, -1), out_map(-c_im).reshape(n_strip, LANES, -1)],
                    axis=1).reshape(2 * g * p, -1)
    pr = ab_re.reshape(1, g * p)
    pi = ab_im.reshape(1, g * p)
    s = 1
    while s < chunk:
        lr, li = pr[s - 1:s], pi[s - 1:s]
        pr, pi = (jnp.concatenate([pr, pr * lr - pi * li], axis=0),
                  jnp.concatenate([pi, pr * li + pi * lr], axis=0))
        s *= 2
    run = chunk // SUBLANES
    run_rows = jnp.arange(1, SUBLANES) * run - 1
    mur = jnp.concatenate([jnp.ones((1, g * p), F32), pr[run_rows]], axis=0)
    mui = jnp.concatenate([jnp.zeros((1, g * p), F32), pi[run_rows]], axis=0)
    return bbd.astype(BF16), pr, pi, mur, mui, cbd.astype(BF16)


def _s5(rest, batch, seq, col_block, width, tables, d, w_glu_bf, b_glu, norm_g, chunk):
    bbd, pwr, pwi, mur, mui, cbd = tables
    nstate = pwr.shape[1]
    rest_v = rest.reshape(batch, seq, rest.shape[-1])

    def full(shape):
        return pl.BlockSpec(shape, lambda b, n: (0,) * len(shape))

    nb = min(S5_SEQS_PER_STEP, batch)
    out = pl.pallas_call(
        functools.partial(_s5_kernel, nstate=nstate),
        grid=(batch // nb, seq // chunk),
        in_specs=[
            pl.BlockSpec((nb, chunk, width), lambda b, n: (b, n, col_block)),
            full((width, 2 * nstate)),
            full((chunk, nstate)),
            full((chunk, nstate)),
            full((SUBLANES, nstate)),
            full((SUBLANES, nstate)),
            full((2 * nstate, width)),
            full((1, width)),
            full((width, width)),
            full((1, width)),
            full((1, width)),
        ],
        out_specs=pl.BlockSpec((nb, chunk, width), lambda b, n: (b, n, 0)),
        out_shape=jax.ShapeDtypeStruct((batch, seq, width), BF16),
        scratch_shapes=[pltpu.VMEM((nb, 1, nstate), F32)] * 2 + [
            pltpu.VMEM((nb, width // LANES, SUBLANES * _run_pitch(chunk // SUBLANES), LANES), F32),
            pltpu.VMEM((nb, chunk, 2 * nstate), F32),
            pltpu.VMEM((nb, chunk, 2 * nstate), BF16),
        ],
        compiler_params=_params("parallel", "arbitrary"),
        name="s5_scan",
    )(rest_v, bbd, pwr, pwi, mur, mui, cbd, d.reshape(1, width), w_glu_bf, b_glu.reshape(1, width),
      norm_g.reshape(1, width))
    return out.reshape(batch * seq, width)


def _layer_norm_rows(y, g, b):
    mu = jnp.mean(y, axis=-1, keepdims=True)
    yc = y - mu
    var = jnp.mean(yc * yc, axis=-1, keepdims=True)
    return yc * lax.rsqrt(var + LN_EPS) * g + b


def _mix_kernel(*refs, dilations):
    nb = len(dilations)
    o_refs, l_refs = refs[:nb], refs[nb:2 * nb]
    lru_ref, s5_ref, h_ref, w_ref, ga_ref, lg_ref, lb_ref, out_ref, tok_ref = refs[2 * nb:]
    npair = o_refs[0].shape[1]
    tm = h_ref.shape[0]

    part = tm // MIX_ROW_PARTS

    def token_major(ref, d, slot, p, r0):
        if d == 1:
            return ref[0, p, r0:r0 + part, :].astype(F32)
        for c in range(d):
            tok_ref[slot, pl.ds(r0 + c, part // d, stride=d), :] = (
                ref[c, p, r0 // d:(r0 + part) // d, :].astype(F32))
        return tok_ref[slot, r0:r0 + part, :]

    ga = ga_ref[...]
    for part_idx in range(MIX_ROW_PARTS):
        r0 = part_idx * part
        rows = slice(r0, r0 + part)
        attn = []
        ssq = jnp.zeros((part, 1), F32)
        for p in range(npair):
            outs = [token_major(o_refs[i], d, 2 * (nb * p + i), p, r0) for i, d in enumerate(dilations)]
            lses = [token_major(l_refs[i], d, 2 * (nb * p + i) + 1, p, r0) for i, d in enumerate(dilations)]
            m = functools.reduce(jnp.maximum, lses)
            es = [jnp.exp(l - m) for l in lses]
            a = sum(e * o for e, o in zip(es, outs)) / sum(es)
            ssq = ssq + jnp.sum(a * a, axis=-1, keepdims=True)
            attn.append(a)
        scale = lax.rsqrt(ssq * (1.0 / (npair * HEAD_PAIR)) + RMS_EPS)
        attn_n = [(a * scale * ga[:, p * HEAD_PAIR:(p + 1) * HEAD_PAIR]).astype(BF16) for p, a in enumerate(attn)]
        mixed = jnp.concatenate(attn_n + [lru_ref[rows, :], s5_ref[rows, :]], axis=1)
        mix = jnp.dot(mixed, w_ref[...], preferred_element_type=F32)
        y = DEEPNORM_ALPHA * h_ref[rows, :] + mix
        out_ref[rows, :] = _layer_norm_rows(y, lg_ref[...], lb_ref[...])


def _mix(attn_outs, dilations, lru, s5, h, seq, w_out_bf, g_attn, ln_g, ln_b, tm):
    t, d = h.shape
    aw = lru.shape[1]
    sw = s5.shape[1]
    nseq = seq // tm
    npair = aw // HEAD_PAIR

    def rows(w):
        return pl.BlockSpec((tm, w), lambda i: (i, 0))

    def full(shape):
        return pl.BlockSpec(shape, lambda i: (0,) * len(shape))

    def class_major(dil):
        return pl.BlockSpec((None, dil, npair, tm // dil, HEAD_PAIR),
                            lambda i: (i // nseq, 0, 0, i % nseq, 0))

    branch_specs = [class_major(dil) for dil in dilations]
    return pl.pallas_call(
        functools.partial(_mix_kernel, dilations=tuple(dilations)),
        grid=(t // tm,),
        in_specs=branch_specs + branch_specs + [rows(aw), rows(sw), rows(d), full((d, d)),
                                                full((1, aw)), full((1, d)), full((1, d))],
        out_specs=rows(d),
        out_shape=jax.ShapeDtypeStruct((t, d), F32),
        scratch_shapes=[pltpu.VMEM((2 * len(dilations) * npair, tm + SUBLANES, HEAD_PAIR), F32)],
        compiler_params=_params("parallel"),
        name="mix_out_ln",
    )(*[o for o, _ in attn_outs], *[l for _, l in attn_outs], lru, s5, h, w_out_bf,
      g_attn.reshape(1, aw), ln_g.reshape(1, d), ln_b.reshape(1, d))


def _ffn_kernel(h_ref, hp_ref, wu_ref, cw_ref, cb_ref, wd_ref, lg_ref, lb_ref, out_ref,
                *, tiles_per_seq, ff_chunk):
    i = pl.program_id(0)
    keep_prev = jnp.where(i % tiles_per_seq == 0, 0.0, 1.0).astype(F32)
    h = h_ref[...]
    he = jnp.concatenate([hp_ref[...] * keep_prev, h], axis=0).astype(BF16)
    d_ff = wd_ref.shape[0]

    def conv_branch(col0):
        cols = slice(col0, col0 + ff_chunk)
        up = jnp.dot(he, wu_ref[:, cols], preferred_element_type=F32)
        cw = cw_ref[:, cols]
        y = cw[FFN_CONV - 1:FFN_CONV, :] * up
        for k in range(1, FFN_CONV):
            tap = FFN_CONV - 1 - k
            y = y + cw[tap:tap + 1, :] * _shift_rows(up, k)
        return y[SUBLANES:, :] + cb_ref[:, cols]

    acc = None
    for j in range(d_ff // ff_chunk):
        gate = conv_branch(j * ff_chunk)
        val = conv_branch(d_ff + j * ff_chunk)
        act = (_gelu(gate) * val).astype(BF16)
        part = jnp.dot(act, wd_ref[j * ff_chunk:(j + 1) * ff_chunk, :], preferred_element_type=F32)
        acc = part if acc is None else acc + part
    y = DEEPNORM_ALPHA * h + acc
    out_ref[...] = _layer_norm_rows(y, lg_ref[...], lb_ref[...])


def _ffn(h, seq, w_up_bf, conv_w, conv_b, w_down_bf, ln_g, ln_b, tm, cw):
    t, d = h.shape
    d_ff = w_down_bf.shape[0]
    per8 = tm // SUBLANES

    def resident(shape):
        return pl.BlockSpec(shape, lambda i: (0,) * len(shape), pipeline_mode=pl.Buffered(1))

    return pl.pallas_call(
        functools.partial(_ffn_kernel, tiles_per_seq=seq // tm, ff_chunk=cw),
        grid=(t // tm,),
        in_specs=[
            pl.BlockSpec((tm, d), lambda i: (i, 0)),
            pl.BlockSpec((SUBLANES, d), lambda i: (jnp.maximum(i * per8 - 1, 0), 0)),
            resident((d, 2 * d_ff)),
            resident((FFN_CONV, 2 * d_ff)),
            resident((1, 2 * d_ff)),
            resident((d_ff, d)),
            resident((1, d)),
            resident((1, d)),
        ],
        out_specs=pl.BlockSpec((tm, d), lambda i: (i, 0)),
        out_shape=jax.ShapeDtypeStruct((t, d), F32),
        compiler_params=pltpu.CompilerParams(dimension_semantics=("parallel",),
                                             vmem_limit_bytes=FFN_VMEM_LIMIT),
        name="conv_ffn_ln",
    )(h, h, w_up_bf, conv_w, conv_b.reshape(1, 2 * d_ff), w_down_bf,
      ln_g.reshape(1, d), ln_b.reshape(1, d))


def kernel(x, w_in, lru_conv_w, lru_conv_b, lru_wr, lru_br, lru_wi, lru_bi, lru_lambda, s5_a_re, s5_a_im, s5_b_re, s5_b_im, s5_c_re, s5_c_im, s5_d, s5_log_step, s5_w_glu, s5_b_glu, mix_norm_g, w_out, ln1_g, ln1_b, w_up, ffn_conv_w, ffn_conv_b, w_down, ln2_g, ln2_b):
    batch, seq, d_model = x.shape
    depth = w_in.shape[0]
    lru_width = lru_conv_w.shape[-1]
    s5_width = s5_w_glu.shape[-1]
    attn_width = (w_in.shape[-1] - 2 * lru_width - s5_width) // 3
    assert attn_width % HEAD_PAIR == 0 and lru_width == attn_width
    assert all(w // d == ATTN_BLOCK for w, d in DILATED_PAIRS)
    dilations = [d for _, d in DILATED_PAIRS]
    s5_col_block = (2 * lru_width) // s5_width
    assert s5_col_block * s5_width == 2 * lru_width

    cos_t, sin_t = _rope_tables(seq)
    all_tables = jax.vmap(functools.partial(_s5_tables, chunk=S5_CHUNK))(
        s5_a_re, s5_a_im, s5_b_re, s5_b_im, s5_c_re, s5_c_im, s5_log_step)
    all_w_gate = jnp.concatenate([jax.vmap(_block_diag)(lru_wr), jax.vmap(_block_diag)(lru_wi)], axis=2).astype(BF16)
    all_b_gate = jnp.concatenate([lru_br, lru_bi], axis=1)
    h = x.reshape(batch * seq, d_model)
    for l in range(depth):
        qkvs, rest = _proj(h, w_in[l].astype(BF16), cos_t, sin_t, batch, attn_width, dilations, PROJ_TILE)
        attn_outs = [_attn_branch(qkv_cm, d) for qkv_cm, d in zip(qkvs, dilations)]
        lru = _lru(rest, batch, seq, lru_width, lru_conv_w[l], lru_conv_b[l], all_w_gate[l], all_b_gate[l],
                   lru_lambda[l], mix_norm_g[l, attn_width:attn_width + lru_width], LRU_CHUNK)
        tables = tuple(t[l] for t in all_tables)
        s5 = _s5(rest, batch, seq, s5_col_block, s5_width, tables, s5_d[l].reshape(-1),
                 s5_w_glu[l].astype(BF16), s5_b_glu[l], mix_norm_g[l, attn_width + lru_width:], S5_CHUNK)
        h1 = _mix(attn_outs, dilations, lru, s5, h, seq, w_out[l].astype(BF16), mix_norm_g[l, :attn_width],
                  ln1_g[l], ln1_b[l], MIX_TILE)
        h = _ffn(h1, seq, w_up[l].astype(BF16), ffn_conv_w[l], ffn_conv_b[l], w_down[l].astype(BF16),
                 ln2_g[l], ln2_b[l], FFN_TILE, FFN_CHUNK)
    return h.reshape(batch, seq, d_model)
```
